```python
import math
import jax
import jax.numpy as jnp
from jax import lax
import numpy as np


D_MODEL = 4096
BATCH = 4
SEQ = 4096
DEPTH = 4

CTX_LEN = 256
GRID_W = 64
HEAD_DIM = 128
N_HEADS_TOTAL = D_MODEL // HEAD_DIM
A_HEADS = 3 * N_HEADS_TOTAL // 8
A_KV_HEADS = A_HEADS // 3
A_GROUP = A_HEADS // A_KV_HEADS
B_HEADS = N_HEADS_TOTAL // 4
B_SUB = HEAD_DIM // 2
C_HEADS = N_HEADS_TOTAL - A_HEADS - B_HEADS
C_KV_HEADS = C_HEADS // 3
C_GROUP = C_HEADS // C_KV_HEADS
MIX_WIDTH = N_HEADS_TOTAL * HEAD_DIM
PROJ_SIZES = (A_HEADS * HEAD_DIM, A_KV_HEADS * HEAD_DIM, A_KV_HEADS * HEAD_DIM,
              B_HEADS * HEAD_DIM, B_HEADS * HEAD_DIM, B_HEADS * HEAD_DIM,
              C_HEADS * HEAD_DIM, C_KV_HEADS * HEAD_DIM, C_KV_HEADS * HEAD_DIM)
PROJ_WIDTH = sum(PROJ_SIZES)
WINDOW = 128
Q_BLOCK = 128
ROPE_THETA = 10000.0
ATT_SCALE = HEAD_DIM ** -0.5
DIFF_SCALE = B_SUB ** -0.5
N_GROUPS = 4
EXPERTS_PER_GROUP = 4
N_EXPERTS = N_GROUPS * EXPERTS_PER_GROUP
TOP_K_IN_GROUP = 2
EXPERT_FF = 3 * D_MODEL // 32
N_MOD = 6
NORM_EPS = 1e-6
NEG_INF = -1e30

kernel_name = "hybrid_dit_parallel_heads_hmoe"


def rmsnorm(x, g):
    xf = x.astype(jnp.float32)
    xf = xf * lax.rsqrt(jnp.mean(xf * xf, axis=-1, keepdims=True) + NORM_EPS)
    return (xf * g.astype(jnp.float32)).astype(x.dtype)


def modulate(x, shift, scale):
    return x * (1 + scale) + shift


def adaln(cond, w, b):
    m = jax.nn.silu(cond) @ w + b
    d = cond.shape[-1]
    return tuple(m[:, None, i * d:(i + 1) * d] for i in range(N_MOD))


def rope_1d(x, pos):
    d = x.shape[-1]
    inv_freq = ROPE_THETA ** (-jnp.arange(0, d, 2, dtype=jnp.float32) / d)
    ang = pos.astype(jnp.float32)[:, None] * inv_freq
    ang = ang.reshape((1, pos.shape[0]) + (1,) * (x.ndim - 3) + (d // 2,))
    cos, sin = jnp.cos(ang), jnp.sin(ang)
    xf = x.astype(jnp.float32)
    x1, x2 = xf[..., :d // 2], xf[..., d // 2:]
    return jnp.concatenate([x1 * cos - x2 * sin, x2 * cos + x1 * sin], axis=-1).astype(x.dtype)


def axial_rope(x, row, col):
    h = x.shape[-1] // 2
    return jnp.concatenate([rope_1d(x[..., :h], row), rope_1d(x[..., h:], col)], axis=-1)


def split_proj(p):
    bx, n = p.shape[:2]
    points, acc = [], 0
    for s in PROJ_SIZES[:-1]:
        acc += s
        points.append(acc)
    qa, ka, va, qb, kb, vb, qc, kc, vc = jnp.split(p, points, axis=-1)
    return (qa.reshape(bx, n, A_KV_HEADS, A_GROUP, HEAD_DIM),
            ka.reshape(bx, n, A_KV_HEADS, HEAD_DIM),
            va.reshape(bx, n, A_KV_HEADS, HEAD_DIM),
            qb.reshape(bx, n, B_HEADS, 2, B_SUB),
            kb.reshape(bx, n, B_HEADS, 2, B_SUB),
            vb.reshape(bx, n, B_HEADS, HEAD_DIM),
            qc.reshape(bx, n, C_KV_HEADS, C_GROUP, HEAD_DIM),
            kc.reshape(bx, n, C_KV_HEADS, HEAD_DIM),
            vc.reshape(bx, n, C_KV_HEADS, HEAD_DIM))


def sweep_query_blocks(attend, q):
    bx, n = q.shape[:2]
    nb = n // Q_BLOCK
    qb = jnp.moveaxis(q.reshape((bx, nb, Q_BLOCK) + q.shape[2:]), 1, 0)
    out = lax.map(attend, (qb, jnp.arange(nb)))
    return jnp.moveaxis(out, 0, 1).reshape((bx, n) + out.shape[3:])


def gqa_attend(q, k, v, sink=None):
    s = jnp.einsum('bqhgd,bkhd->bhgqk', q, k).astype(jnp.float32) * ATT_SCALE
    if sink is not None:
        s_sink = jnp.broadcast_to(sink.astype(jnp.float32)[None, :, :, None, None], s.shape[:-1] + (1,))
        p = jax.nn.softmax(jnp.concatenate([s, s_sink], axis=-1), axis=-1)[..., :-1]
    else:
        p = jax.nn.softmax(s, axis=-1)
    return jnp.einsum('bhgqk,bkhd->bqhgd', p.astype(v.dtype), v)


def windowed_gqa_sink(q, k, v, k_ctx, v_ctx, sink):
    n = k.shape[1]
    halo = ((0, 0), (Q_BLOCK, Q_BLOCK), (0, 0), (0, 0))
    k_pad, v_pad = jnp.pad(k, halo), jnp.pad(v, halo)
    qi = jnp.arange(Q_BLOCK)[:, None]
    kj = jnp.arange(3 * Q_BLOCK)[None, :]
    in_window = jnp.abs(kj - Q_BLOCK - qi) <= WINDOW

    def attend(blk):
        q_blk, nb_idx = blk
        start = nb_idx * Q_BLOCK
        k_win = lax.dynamic_slice_in_dim(k_pad, start, 3 * Q_BLOCK, axis=1)
        v_win = lax.dynamic_slice_in_dim(v_pad, start, 3 * Q_BLOCK, axis=1)
        k_pos = start - Q_BLOCK + kj
        allowed = in_window & (k_pos >= 0) & (k_pos < n)
        s_loc = jnp.einsum('bqhgd,bkhd->bhgqk', q_blk, k_win).astype(jnp.float32) * ATT_SCALE
        s_loc = jnp.where(allowed, s_loc, NEG_INF)
        s_ctx = jnp.einsum('bqhgd,bkhd->bhgqk', q_blk, k_ctx).astype(jnp.float32) * ATT_SCALE
        s_sink = jnp.broadcast_to(sink.astype(jnp.float32)[None, :, :, None, None], s_ctx.shape[:-1] + (1,))
        p = jax.nn.softmax(jnp.concatenate([s_loc, s_ctx, s_sink], axis=-1), axis=-1)
        p_loc = p[..., :3 * Q_BLOCK].astype(v.dtype)
        p_ctx = p[..., 3 * Q_BLOCK:-1].astype(v.dtype)
        return (jnp.einsum('bhgqk,bkhd->bqhgd', p_loc, v_win)
                + jnp.einsum('bhgqk,bkhd->bqhgd', p_ctx, v_ctx))

    return sweep_query_blocks(attend, q)


def diff_attend(q, k, v, lam):
    s = jnp.einsum('bqhmd,bkhmd->bhmqk', q, k).astype(jnp.float32) * DIFF_SCALE
    p = jax.nn.softmax(s, axis=-1)
    w = p[:, :, 0] - lam * p[:, :, 1]
    return jnp.einsum('bhqk,bkhd->bqhd', w.astype(v.dtype), v)


def diff_post(o, g, lam_init):
    return rmsnorm(o, g) * (1.0 - lam_init)


def hier_moe(v, w_rg, b_rg, w_re, b_re, w_gate, w_up, w_down):
    shape = v.shape
    t = v.reshape(-1, shape[-1])
    lg = (t @ w_rg + b_rg).astype(jnp.float32)
    p_grp, g_idx = lax.top_k(jax.nn.softmax(lg, axis=-1), 1)
    le = (jnp.einsum('nd,gde->nge', t, w_re) + b_re).astype(jnp.float32)
    le_sel = jnp.einsum('nge,ng->ne', le, jax.nn.one_hot(g_idx[:, 0], N_GROUPS, dtype=jnp.float32))
    top_l, top_i = lax.top_k(le_sel, TOP_K_IN_GROUP)
    w_sel = jax.nn.softmax(top_l, axis=-1) * p_grp
    ids = g_idx * EXPERTS_PER_GROUP + top_i
    combine = jnp.einsum('nk,nke->ne', w_sel, jax.nn.one_hot(ids, N_EXPERTS, dtype=jnp.float32))
    h = jax.nn.silu(jnp.einsum('nd,edf->nef', t, w_gate)) * jnp.einsum('nd,edf->nef', t, w_up)
    y = jnp.einsum('nef,efd->nd', h * combine[:, :, None].astype(h.dtype), w_down)
    return y.reshape(shape)


def setup_inputs(seed: int = 0) -> dict:
    key = jax.random.key(seed)
    ks = jax.random.split(key, 26)
    f32 = jnp.float32
    L, D = DEPTH, D_MODEL

    def nrm(k, shape, s):
        return jax.random.normal(k, shape, f32) * s

    return {
        "x": nrm(ks[0], (BATCH, SEQ, D), 1.0),
        "c": nrm(ks[1], (BATCH, D), 1.0),
        "ctx": nrm(ks[2], (BATCH, CTX_LEN, D), 1.0),
        "c_ctx": nrm(ks[3], (D,), 1.0),
        "ada_w": nrm(ks[4], (L, D, N_MOD * D), 0.5 * D ** -0.5),
        "ada_b": nrm(ks[5], (L, N_MOD * D), 0.02),
        "norm_mix": 1.0 + nrm(ks[6], (L, D), 0.02),
        "w_in": nrm(ks[7], (L, D, PROJ_WIDTH), D ** -0.5),
        "sink_a": nrm(ks[8], (L, A_HEADS), 0.5),
        "lambda_q1": nrm(ks[9], (L, B_SUB), 0.1),
        "lambda_k1": nrm(ks[10], (L, B_SUB), 0.1),
        "lambda_q2": nrm(ks[11], (L, B_SUB), 0.1),
        "lambda_k2": nrm(ks[12], (L, B_SUB), 0.1),
        "subln_b": 1.0 + nrm(ks[13], (L, HEAD_DIM), 0.02),
        "q_norm_c": 1.0 + nrm(ks[14], (L, HEAD_DIM), 0.02),
        "k_norm_c": 1.0 + nrm(ks[15], (L, HEAD_DIM), 0.02),
        "w_out": nrm(ks[16], (L, MIX_WIDTH, D), MIX_WIDTH ** -0.5),
        "norm_ffn": 1.0 + nrm(ks[17], (L, D), 0.02),
        "w_router_grp": nrm(ks[18], (L, D, N_GROUPS), D ** -0.5),
        "b_router_grp": nrm(ks[19], (L, N_GROUPS), 0.01),
        "w_router_exp": nrm(ks[20], (L, N_GROUPS, D, EXPERTS_PER_GROUP), D ** -0.5),
        "b_router_exp": nrm(ks[21], (L, N_GROUPS, EXPERTS_PER_GROUP), 0.01),
        "w_gate": nrm(ks[22], (L, N_EXPERTS, D, EXPERT_FF), D ** -0.5),
        "w_up": nrm(ks[23], (L, N_EXPERTS, D, EXPERT_FF), D ** -0.5),
        "w_down": nrm(ks[24], (L, N_EXPERTS, EXPERT_FF, D), EXPERT_FF ** -0.5),
        "final_norm": 1.0 + nrm(ks[25], (D,), 0.02),
    }


def reference(x, c, ctx, c_ctx, ada_w, ada_b, norm_mix, w_in, sink_a, lambda_q1, lambda_k1,
              lambda_q2, lambda_k2, subln_b, q_norm_c, k_norm_c, w_out, norm_ffn, w_router_grp,
              b_router_grp, w_router_exp, b_router_exp, w_gate, w_up, w_down, final_norm):
    bx, n_lat, _ = x.shape
    rows = n_lat // GRID_W
    row = jnp.repeat(jnp.arange(rows), GRID_W)
    col = jnp.tile(jnp.arange(GRID_W), rows)
    xc = ctx
    for l in range(DEPTH):
        last = l == DEPTH - 1
        lam_init = 0.8 - 0.6 * math.exp(-0.3 * l)
        lam = (jnp.exp(jnp.sum(lambda_q1[l].astype(jnp.float32) * lambda_k1[l].astype(jnp.float32)))
               - jnp.exp(jnp.sum(lambda_q2[l].astype(jnp.float32) * lambda_k2[l].astype(jnp.float32)))
               + lam_init)
        sh_a, sc_a, g_a, sh_f, sc_f, g_f = adaln(c, ada_w[l], ada_b[l])
        csh_a, csc_a, cg_a, csh_f, csc_f, cg_f = adaln(c_ctx[None], ada_w[l], ada_b[l])

        u = modulate(rmsnorm(x, norm_mix[l]), sh_a, sc_a)
        uc = modulate(rmsnorm(xc, norm_mix[l]), csh_a, csc_a)
        qa, ka, va, qb, kb, vb, qc, kc, vc = split_proj(u @ w_in[l])
        qa_c, ka_c, va_c, qb_c, kb_c, vb_c, qc_c, kc_c, vc_c = split_proj(uc @ w_in[l])
        sink = sink_a[l].reshape(A_KV_HEADS, A_GROUP)

        oa = windowed_gqa_sink(axial_rope(qa, row, col), axial_rope(ka, row, col), va, ka_c, va_c, sink)

        kb_all = jnp.concatenate([axial_rope(kb, row, col), kb_c], axis=1)
        vb_all = jnp.concatenate([vb, vb_c], axis=1)
        ob = sweep_query_blocks(lambda blk: diff_attend(blk[0], kb_all, vb_all, lam),
                                axial_rope(qb, row, col))
        ob = diff_post(ob, subln_b[l], lam_init)

        kc_c_n = rmsnorm(kc_c, k_norm_c[l])
        kc_all = jnp.concatenate([axial_rope(rmsnorm(kc, k_norm_c[l]), row, col), kc_c_n], axis=1)
        vc_all = jnp.concatenate([vc, vc_c], axis=1)
        oc = sweep_query_blocks(lambda blk: gqa_attend(blk[0], kc_all, vc_all),
                                axial_rope(rmsnorm(qc, q_norm_c[l]), row, col))

        mix = jnp.concatenate([oa.reshape(bx, n_lat, -1), ob.reshape(bx, n_lat, -1),
                               oc.reshape(bx, n_lat, -1)], axis=-1)
        x = x + g_a * (mix @ w_out[l])

        if not last:
            oa_c = gqa_attend(qa_c, ka_c, va_c, sink)
            ob_c = diff_post(diff_attend(qb_c, kb_c, vb_c, lam), subln_b[l], lam_init)
            oc_c = gqa_attend(rmsnorm(qc_c, q_norm_c[l]), kc_c_n, vc_c)
            n_ctx = xc.shape[1]
            mix_c = jnp.concatenate([oa_c.reshape(bx, n_ctx, -1), ob_c.reshape(bx, n_ctx, -1),
                                     oc_c.reshape(bx, n_ctx, -1)], axis=-1)
            xc = xc + cg_a * (mix_c @ w_out[l])

        moe_w = (w_router_grp[l], b_router_grp[l], w_router_exp[l], b_router_exp[l],
                 w_gate[l], w_up[l], w_down[l])
        x = x + g_f * hier_moe(modulate(rmsnorm(x, norm_ffn[l]), sh_f, sc_f), *moe_w)
        if not last:
            xc = xc + cg_f * hier_moe(modulate(rmsnorm(xc, norm_ffn[l]), csh_f, csc_f), *moe_w)

    return rmsnorm(x, final_norm)
```

```python
import functools
import math

import jax
import jax.numpy as jnp
from jax import lax
from jax.experimental import pallas as pl
from jax.experimental.pallas import tpu as pltpu

F32 = jnp.float32
BF16 = jnp.bfloat16

HEAD_DIM = 128
GRID_W = 64
WINDOW = 128
ROPE_THETA = 10000.0
ATT_SCALE = HEAD_DIM ** -0.5
DIFF_SCALE = (HEAD_DIM // 2) ** -0.5
N_GROUPS = 4
EXPERTS_PER_GROUP = 4
N_EXPERTS = N_GROUPS * EXPERTS_PER_GROUP
N_MOD = 6
NORM_EPS = 1e-6
NEG_INF = -1e30
ROUTER_LANES = 128
EXPERT_LANE0 = N_GROUPS

V7X_VMEM_LIMIT_BYTES = 52 * 1024 * 1024


def _cparams(*sem):
    return pltpu.CompilerParams(dimension_semantics=sem, vmem_limit_bytes=V7X_VMEM_LIMIT_BYTES)


def _pick(n, *cands):
    for c in cands:
        if n % c == 0:
            return c
    return n


def _adaln_body(cond_ref, w_ref, b_ref, o_ref):
    c = cond_ref[...]
    s = (c * jax.nn.sigmoid(c)).astype(BF16)
    o_ref[...] = jnp.dot(s, w_ref[...].astype(BF16), preferred_element_type=F32) + b_ref[...]


def adaln_all(cond, ada_w, ada_b):
    n_layers, d, n6 = ada_w.shape
    r = cond.shape[0]
    tn = _pick(n6, 512, 256, 128)
    return pl.pallas_call(
        _adaln_body,
        grid=(n_layers, n6 // tn),
        in_specs=[
            pl.BlockSpec((r, d), lambda l, j: (0, 0)),
            pl.BlockSpec((None, d, tn), lambda l, j: (l, 0, j)),
            pl.BlockSpec((None, 1, tn), lambda l, j: (l, 0, j)),
        ],
        out_specs=pl.BlockSpec((None, r, tn), lambda l, j: (l, 0, j)),
        out_shape=jax.ShapeDtypeStruct((n_layers, r, n6), F32),
        compiler_params=_cparams("arbitrary", "arbitrary"),
        name="adaln",
    )(cond, ada_w, ada_b.reshape(n_layers, 1, n6))


def _norm_body(*refs, has_res, has_mod, has_router):
    it = iter(refs)
    x_ref = next(it)
    if has_res:
        y_ref, gate_ref = next(it), next(it)
    g_ref = next(it)
    if has_mod:
        sh_ref, sc_ref = next(it), next(it)
    if has_router:
        wh_ref, wl_ref, rb_ref = next(it), next(it), next(it)
    if has_res:
        xo_ref = next(it)
    u_ref = next(it)
    if has_router:
        comb_ref = next(it)

    x = x_ref[...]
    if has_res:
        x = x + gate_ref[...] * y_ref[...].astype(F32)
        xo_ref[...] = x
    xn = x * lax.rsqrt(jnp.mean(x * x, axis=-1, keepdims=True) + NORM_EPS) * g_ref[...]
    if has_mod:
        xn = xn * (1.0 + sc_ref[...]) + sh_ref[...]
    u_ref[...] = xn.astype(u_ref.dtype)

    if has_router:
        th = xn.astype(BF16)
        tl = (xn - th.astype(F32)).astype(BF16)
        wh = wh_ref[...]
        logits = (jnp.dot(th, wh, preferred_element_type=F32)
                  + jnp.dot(tl, wh, preferred_element_type=F32)
                  + jnp.dot(th, wl_ref[...], preferred_element_type=F32)) + rb_ref[...]
        lane = lax.broadcasted_iota(jnp.int32, logits.shape, 1)
        big = jnp.int32(ROUTER_LANES)
        lg = jnp.where(lane < N_GROUPS, logits, -jnp.inf)
        gmax = jnp.max(lg, axis=-1, keepdims=True)
        p_grp = 1.0 / jnp.sum(jnp.exp(lg - gmax), axis=-1, keepdims=True)
        gidx = jnp.min(jnp.where(lg == gmax, lane, big), axis=-1, keepdims=True)
        e_lo = EXPERT_LANE0 + EXPERTS_PER_GROUP * gidx
        le = jnp.where((lane >= e_lo) & (lane < e_lo + EXPERTS_PER_GROUP), logits, -jnp.inf)
        l1 = jnp.max(le, axis=-1, keepdims=True)
        i1 = jnp.min(jnp.where(le == l1, lane, big), axis=-1, keepdims=True)
        le2 = jnp.where(lane == i1, -jnp.inf, le)
        l2 = jnp.max(le2, axis=-1, keepdims=True)
        i2 = jnp.min(jnp.where(le2 == l2, lane, big), axis=-1, keepdims=True)
        e2 = jnp.exp(l2 - l1)
        w1 = p_grp / (1.0 + e2)
        w2 = w1 * e2
        comb = jnp.where(lane == i1, w1, jnp.where(lane == i2, w2, 0.0))
        comb_ref[...] = jnp.where(lane == 0, gidx.astype(F32), comb)


def norm_stage(x, gain, *, res=None, mod=None, router=None, out_dtype=BF16):
    b, t, d = x.shape
    ts = _pick(t, 256, 128, 64, 32, 16, 8)
    row = pl.BlockSpec((None, ts, d), lambda bi, i: (bi, i, 0))

    def per_batch(arr):
        if arr.shape[0] == b:
            return pl.BlockSpec((None, 1, d), lambda bi, i: (bi, 0, 0))
        return pl.BlockSpec((None, 1, d), lambda bi, i: (0, 0, 0))

    args, specs = [x], [row]
    if res is not None:
        y, gate = res
        args += [y, gate]
        specs += [row, per_batch(gate)]
    args.append(gain.reshape(1, d))
    specs.append(pl.BlockSpec((1, d), lambda bi, i: (0, 0)))
    if mod is not None:
        args += [mod[0], mod[1]]
        specs += [per_batch(mod[0]), per_batch(mod[1])]
    if router is not None:
        args += list(router)
        specs += [pl.BlockSpec((d, ROUTER_LANES), lambda bi, i: (0, 0)),
                  pl.BlockSpec((d, ROUTER_LANES), lambda bi, i: (0, 0)),
                  pl.BlockSpec((1, ROUTER_LANES), lambda bi, i: (0, 0))]
    out_shapes, out_specs = [], []
    if res is not None:
        out_shapes.append(jax.ShapeDtypeStruct((b, t, d), F32))
        out_specs.append(row)
    out_shapes.append(jax.ShapeDtypeStruct((b, t, d), out_dtype))
    out_specs.append(row)
    if router is not None:
        out_shapes.append(jax.ShapeDtypeStruct((b, t, ROUTER_LANES), F32))
        out_specs.append(pl.BlockSpec((None, ts, ROUTER_LANES), lambda bi, i: (bi, i, 0)))
    body = functools.partial(_norm_body, has_res=res is not None, has_mod=mod is not None,
                             has_router=router is not None)
    return pl.pallas_call(
        body, grid=(b, t // ts), in_specs=specs, out_specs=out_specs, out_shape=out_shapes,
        compiler_params=_cparams("arbitrary", "arbitrary"), name="norm_stage",
    )(*args)


def _rope_tables(seq, block):
    half = block // 2
    lane = jnp.arange(HEAD_DIM)
    within = lane % block
    freq = (within % half).astype(F32)
    inv = ROPE_THETA ** (-(2.0 * freq) / block)
    t = jnp.arange(seq)
    pos = jnp.where(((lane // block) % 2 == 0)[None, :], (t // GRID_W)[:, None], (t % GRID_W)[:, None])
    ang = pos.astype(F32) * inv[None, :]
    cos, sin = jnp.cos(ang), jnp.sin(ang)
    first = (within < half)[None, :]
    return jnp.stack([cos, jnp.where(first, -sin, 0.0), jnp.where(first, 0.0, sin)])


def _inproj_body(*refs, segs, rope, heads_per_tile):
    if rope:
        a_ref, w_ref, gq_ref, gk_ref, tab_ac_ref, tab_b_ref, o_ref = refs
    else:
        a_ref, w_ref, gq_ref, gk_ref, o_ref = refs
    acc = jnp.dot(a_ref[...], w_ref[...], preferred_element_type=F32)
    j = pl.program_id(1)

    def emit(kind):
        for h in range(heads_per_tile):
            sl = slice(h * HEAD_DIM, (h + 1) * HEAD_DIM)
            y = acc[:, sl]
            if kind in ("qc", "kc"):
                g = gq_ref[...] if kind == "qc" else gk_ref[...]
                y = y * lax.rsqrt(jnp.mean(y * y, axis=-1, keepdims=True) + NORM_EPS) * g
            if rope and kind != "v":
                if kind in ("qb", "kb"):
                    tab, half = tab_b_ref, HEAD_DIM // 8
                else:
                    tab, half = tab_ac_ref, HEAD_DIM // 4
                y = (y * tab[0] + pltpu.roll(y, HEAD_DIM - half, 1) * tab[1]
                     + pltpu.roll(y, half, 1) * tab[2])
            if kind in ("qa", "qc"):
                y = y * ATT_SCALE
            elif kind == "qb":
                y = y * DIFF_SCALE
            o_ref[:, sl] = y.astype(o_ref.dtype)

    for lo, hi, kind in segs:
        pl.when((j >= lo) & (j < hi))(functools.partial(emit, kind))


def _proj_layout(d):
    h = d // HEAD_DIM
    a_h = 3 * h // 8
    a_kv = a_h // 3
    b_h = h // 4
    c_h = h - a_h - b_h
    c_kv = c_h // 3
    sizes = [a_h, a_kv, a_kv, b_h, b_h, b_h, c_h, c_kv, c_kv]
    names = ["qa", "ka", "va", "qb", "kb", "vb", "qc", "kc", "vc"]
    offs, acc = {}, 0
    for n, s in zip(names, sizes):
        offs[n] = acc
        acc += s
    return dict(zip(names, sizes)), offs, acc


def in_projection(u, w_bf16, q_gain, k_gain, *, rope_tabs=None):
    b, t, d = u.shape
    pw = w_bf16.shape[1]
    sizes, offs, total = _proj_layout(d)
    assert total * HEAD_DIM == pw
    m = b * t
    tm = _pick(t, 1024, 512, 256, 128) if rope_tabs is not None else _pick(m, 1024, 512, 256, 128)
    g = functools.reduce(math.gcd, sizes.values())
    hpt = _pick(g, 4, 2, 1)
    tn = hpt * HEAD_DIM
    kinds = {"qa": "qa", "ka": "ka", "va": "v", "qb": "qb", "kb": "kb", "vb": "v",
             "qc": "qc", "kc": "kc", "vc": "v"}
    segs = tuple((offs[n] // hpt, (offs[n] + sizes[n]) // hpt, kinds[n]) for n in sizes)
    in_specs = [
        pl.BlockSpec((tm, d), lambda i, j: (i, 0)),
        pl.BlockSpec((d, tn), lambda i, j: (0, j)),
        pl.BlockSpec((1, HEAD_DIM), lambda i, j: (0, 0)),
        pl.BlockSpec((1, HEAD_DIM), lambda i, j: (0, 0)),
    ]
    args = [u.reshape(m, d), w_bf16, q_gain.reshape(1, HEAD_DIM), k_gain.reshape(1, HEAD_DIM)]
    if rope_tabs is not None:
        nt = t // tm
        tab_spec = pl.BlockSpec((3, tm, HEAD_DIM), lambda i, j: (0, i % nt, 0))
        in_specs += [tab_spec, tab_spec]
        args += list(rope_tabs)
    body = functools.partial(_inproj_body, segs=segs, rope=rope_tabs is not None, heads_per_tile=hpt)
    out = pl.pallas_call(
        body, grid=(m // tm, pw // tn), in_specs=in_specs,
        out_specs=pl.BlockSpec((tm, tn), lambda i, j: (i, j)),
        out_shape=jax.ShapeDtypeStruct((m, pw), BF16),
        compiler_params=_cparams("arbitrary", "arbitrary"), name="in_projection",
    )(*args)
    return out.reshape(b, t, pw)


def _attn_a_body(sink_ref, q0_ref, q1_ref, q2_ref, kl_ref, vl_ref, kc_ref, vc_ref, o_ref, *, local, tq, wlen):
    hkv = pl.program_id(1)
    qi = pl.program_id(2)
    q = jnp.concatenate([q0_ref[...], q1_ref[...], q2_ref[...]], axis=0)
    nt = (((1,), (1,)), ((), ()))
    s_ctx = lax.dot_general(q, kc_ref[...], nt, preferred_element_type=F32)
    row = lax.broadcasted_iota(jnp.int32, (3 * tq, 1), 0)
    sink = jnp.where(row < tq, sink_ref[3 * hkv],
                     jnp.where(row < 2 * tq, sink_ref[3 * hkv + 1], sink_ref[3 * hkv + 2]))
    m = jnp.maximum(jnp.max(s_ctx, axis=-1, keepdims=True), sink)
    if local:
        n_lat = kl_ref.shape[0]
        start = pl.multiple_of(jnp.clip(qi * tq - WINDOW, 0, n_lat - wlen), WINDOW)
        k_win = kl_ref[pl.ds(start, wlen), :]
        v_win = vl_ref[pl.ds(start, wlen), :]
        s_loc = lax.dot_general(q, k_win, nt, preferred_element_type=F32)
        kpos = start + lax.broadcasted_iota(jnp.int32, (1, wlen), 1)
        qpos = qi * tq + jnp.where(row < tq, row, jnp.where(row < 2 * tq, row - tq, row - 2 * tq))
        s_loc = jnp.where(jnp.abs(kpos - qpos) <= WINDOW, s_loc, NEG_INF)
        m = jnp.maximum(m, jnp.max(s_loc, axis=-1, keepdims=True))
    p_ctx = jnp.exp(s_ctx - m)
    l = jnp.sum(p_ctx, axis=-1, keepdims=True) + jnp.exp(sink - m)
    o = jnp.dot(p_ctx.astype(BF16), vc_ref[...], preferred_element_type=F32)
    if local:
        p_loc = jnp.exp(s_loc - m)
        l = l + jnp.sum(p_loc, axis=-1, keepdims=True)
        o = o + jnp.dot(p_loc.astype(BF16), v_win, preferred_element_type=F32)
    o = o / l
    for g in range(3):
        o_ref[:, g * HEAD_DIM:(g + 1) * HEAD_DIM] = o[g * tq:(g + 1) * tq].astype(o_ref.dtype)


def attention_a(p_q, p_lat, p_ctx, sink, d, *, local):
    sizes, offs, _ = _proj_layout(d)
    b, t, _ = p_q.shape
    n_ctx = p_ctx.shape[1]
    n_lat = p_lat.shape[1]
    tq = _pick(t, 256, 128)
    wlen = tq + 2 * WINDOW
    if local:
        assert n_lat >= wlen
    grid = (b, sizes["ka"], t // tq)

    def qspec(g):
        return pl.BlockSpec((None, tq, HEAD_DIM), lambda bi, h, i: (bi, i, offs["qa"] + 3 * h + g))

    def kvspec(n, name):
        return pl.BlockSpec((None, n, HEAD_DIM), lambda bi, h, i: (bi, 0, offs[name] + h))

    body = functools.partial(_attn_a_body, local=local, tq=tq, wlen=wlen)
    return pl.pallas_call(
        body, grid=grid,
        in_specs=[pl.BlockSpec(memory_space=pltpu.SMEM), qspec(0), qspec(1), qspec(2),
                  kvspec(n_lat, "ka"), kvspec(n_lat, "va"), kvspec(n_ctx, "ka"), kvspec(n_ctx, "va")],
        out_specs=pl.BlockSpec((None, tq, 3 * HEAD_DIM), lambda bi, h, i: (bi, i, h)),
        out_shape=jax.ShapeDtypeStruct((b, t, sizes["qa"] * HEAD_DIM), BF16),
        compiler_params=_cparams("arbitrary", "arbitrary", "arbitrary"), name="attention_a",
    )(sink, p_q, p_q, p_q, p_lat, p_lat, p_ctx, p_ctx)


def _attn_b_body(lq1_ref, lk1_ref, lq2_ref, lk2_ref, gain_ref, q_ref, k_ref, v_ref, o_ref, *, lam_init, tq):
    lam = (jnp.exp(jnp.sum(lq1_ref[...] * lk1_ref[...], axis=-1, keepdims=True))
           - jnp.exp(jnp.sum(lq2_ref[...] * lk2_ref[...], axis=-1, keepdims=True)) + lam_init)
    q = q_ref[...]
    lane = lax.broadcasted_iota(jnp.int32, q.shape, 1)
    zero = jnp.zeros_like(q)
    qq = jnp.concatenate([jnp.where(lane < HEAD_DIM // 2, q, zero),
                          jnp.where(lane < HEAD_DIM // 2, zero, q)], axis=0)
    s = lax.dot_general(qq, k_ref[...], (((1,), (1,)), ((), ())), preferred_element_type=F32)
    m = jnp.max(s, axis=-1, keepdims=True)
    p = jnp.exp(s - m)
    l = jnp.sum(p, axis=-1, keepdims=True)
    o2 = jnp.dot(p.astype(BF16), v_ref[...], preferred_element_type=F32) / l
    o = o2[:tq] - lam * o2[tq:]
    o = o * lax.rsqrt(jnp.mean(o * o, axis=-1, keepdims=True) + NORM_EPS) * gain_ref[...]
    o_ref[...] = (o * (1.0 - lam_init)).astype(o_ref.dtype)


def attention_b(p_q, k_all, v_all, lams, gain, d, *, lam_init):
    sizes, offs, _ = _proj_layout(d)
    b, t, _ = p_q.shape
    nk = k_all.shape[1]
    tq = _pick(t, 256, 128)
    half = HEAD_DIM // 2
    lam_spec = pl.BlockSpec((1, half), lambda bi, h, i: (0, 0))
    kv_spec = pl.BlockSpec((None, nk, HEAD_DIM), lambda bi, h, i: (bi, 0, h))
    body = functools.partial(_attn_b_body, lam_init=lam_init, tq=tq)
    return pl.pallas_call(
        body, grid=(b, sizes["qb"], t // tq),
        in_specs=[lam_spec] * 4 + [pl.BlockSpec((1, HEAD_DIM), lambda bi, h, i: (0, 0)),
                                   pl.BlockSpec((None, tq, HEAD_DIM), lambda bi, h, i: (bi, i, offs["qb"] + h)),
                                   kv_spec, kv_spec],
        out_specs=pl.BlockSpec((None, tq, HEAD_DIM), lambda bi, h, i: (bi, i, h)),
        out_shape=jax.ShapeDtypeStruct((b, t, sizes["qb"] * HEAD_DIM), BF16),
        compiler_params=_cparams("arbitrary", "arbitrary", "arbitrary"), name="attention_b",
    )(*[v.reshape(1, half) for v in lams], gain.reshape(1, HEAD_DIM), p_q, k_all, v_all)


def _attn_c_body(q0_ref, q1_ref, q2_ref, k_ref, v_ref, o_ref, *, tq):
    q = jnp.concatenate([q0_ref[...], q1_ref[...], q2_ref[...]], axis=0)
    s = lax.dot_general(q, k_ref[...], (((1,), (1,)), ((), ())), preferred_element_type=F32)
    m = jnp.max(s, axis=-1, keepdims=True)
    p = jnp.exp(s - m)
    l = jnp.sum(p, axis=-1, keepdims=True)
    o = jnp.dot(p.astype(BF16), v_ref[...], preferred_element_type=F32) / l
    for g in range(3):
        o_ref[:, g * HEAD_DIM:(g + 1) * HEAD_DIM] = o[g * tq:(g + 1) * tq].astype(o_ref.dtype)


def attention_c(p_q, k_all, v_all, d):
    sizes, offs, _ = _proj_layout(d)
    b, t, _ = p_q.shape
    nk = k_all.shape[1]
    tq = _pick(t, 128)

    def qspec(g):
        return pl.BlockSpec((None, tq, HEAD_DIM), lambda bi, h, i: (bi, i, offs["qc"] + 3 * h + g))

    kv_spec = pl.BlockSpec((None, nk, HEAD_DIM), lambda bi, h, i: (bi, 0, h))
    return pl.pallas_call(
        functools.partial(_attn_c_body, tq=tq), grid=(b, sizes["kc"], t // tq),
        in_specs=[qspec(0), qspec(1), qspec(2), kv_spec, kv_spec],
        out_specs=pl.BlockSpec((None, tq, 3 * HEAD_DIM), lambda bi, h, i: (bi, i, h)),
        out_shape=jax.ShapeDtypeStruct((b, t, sizes["qc"] * HEAD_DIM), BF16),
        compiler_params=_cparams("arbitrary", "arbitrary", "arbitrary"), name="attention_c",
    )(p_q, p_q, p_q, k_all, v_all)


def _outproj_body(a_ref, w_ref, x_ref, gate_ref, o_ref):
    acc = jnp.dot(a_ref[...], w_ref[...], preferred_element_type=F32)
    o_ref[...] = x_ref[...] + gate_ref[...] * acc


def out_projection(mix, w_bf16, x, gate):
    b, t, k = mix.shape
    d = w_bf16.shape[1]
    m = b * t
    per_batch = gate.shape[0] == b
    tm = _pick(t, 1024, 512, 256, 128) if per_batch else _pick(m, 1024, 512, 256, 128)
    tn = _pick(d, 512, 256, 128)
    nt = t // tm if per_batch else 1
    gate_map = (lambda i, j: (i // nt, 0, j)) if per_batch else (lambda i, j: (0, 0, j))
    out = pl.pallas_call(
        _outproj_body, grid=(m // tm, d // tn),
        in_specs=[pl.BlockSpec((tm, k), lambda i, j: (i, 0)),
                  pl.BlockSpec((k, tn), lambda i, j: (0, j)),
                  pl.BlockSpec((tm, tn), lambda i, j: (i, j)),
                  pl.BlockSpec((None, 1, tn), gate_map)],
        out_specs=pl.BlockSpec((tm, tn), lambda i, j: (i, j)),
        out_shape=jax.ShapeDtypeStruct((m, d), F32),
        compiler_params=_cparams("arbitrary", "arbitrary"), name="out_projection",
    )(mix.reshape(m, k), w_bf16, x.reshape(m, d), gate)
    return out.reshape(b, t, d)


def _moe_body(t_ref, comb_ref, wg_ref, wu_ref, wd_ref, y_ref, acc_ref):
    e = pl.program_id(1)

    @pl.when(e == 0)
    def _():
        acc_ref[...] = jnp.zeros_like(acc_ref)

    t = t_ref[...]
    g = jnp.dot(t, wg_ref[...], preferred_element_type=F32)
    u = jnp.dot(t, wu_ref[...], preferred_element_type=F32)
    comb = comb_ref[...]
    lane = lax.broadcasted_iota(jnp.int32, comb.shape, 1)
    cw = jnp.sum(jnp.where(lane == EXPERT_LANE0 + e, comb, 0.0), axis=-1, keepdims=True)
    h = (g * jax.nn.sigmoid(g)) * u * cw
    acc_ref[...] += jnp.dot(h.astype(BF16), wd_ref[...], preferred_element_type=F32)

    @pl.when(e == pl.num_programs(1) - 1)
    def _():
        y_ref[...] = acc_ref[...].astype(y_ref.dtype)


def moe_experts(t, comb, wg, wu, wd):
    m, d = t.shape
    n_e, _, ff = wg.shape
    tm = _pick(m, 512, 256, 128)
    return pl.pallas_call(
        _moe_body, grid=(m // tm, n_e),
        in_specs=[pl.BlockSpec((tm, d), lambda i, e: (i, 0)),
                  pl.BlockSpec((tm, ROUTER_LANES), lambda i, e: (i, 0)),
                  pl.BlockSpec((None, d, ff), lambda i, e: (e, 0, 0)),
                  pl.BlockSpec((None, d, ff), lambda i, e: (e, 0, 0)),
                  pl.BlockSpec((None, ff, d), lambda i, e: (e, 0, 0))],
        out_specs=pl.BlockSpec((tm, d), lambda i, e: (i, 0)),
        out_shape=jax.ShapeDtypeStruct((m, d), BF16),
        scratch_shapes=[pltpu.VMEM((tm, d), F32)],
        compiler_params=_cparams("arbitrary", "arbitrary"), name="moe_experts",
    )(t, comb, wg, wu, wd)


def _router_weights(w_rg, b_rg, w_re, b_re):
    d = w_rg.shape[0]
    n_used = N_GROUPS + N_EXPERTS
    w = jnp.concatenate([w_rg, jnp.transpose(w_re, (1, 0, 2)).reshape(d, N_EXPERTS),
                         jnp.zeros((d, ROUTER_LANES - n_used), F32)], axis=1)
    bias = jnp.concatenate([b_rg, b_re.reshape(N_EXPERTS), jnp.zeros((ROUTER_LANES - n_used,), F32)])
    w_hi = w.astype(BF16)
    w_lo = (w - w_hi.astype(F32)).astype(BF16)
    return w_hi, w_lo, bias.reshape(1, ROUTER_LANES)


def kernel(x, c, ctx, c_ctx, ada_w, ada_b, norm_mix, w_in, sink_a, lambda_q1, lambda_k1, lambda_q2,
           lambda_k2, subln_b, q_norm_c, k_norm_c, w_out, norm_ffn, w_router_grp, b_router_grp,
           w_router_exp, b_router_exp, w_gate, w_up, w_down, final_norm):
    bx, n_lat, d = x.shape
    n_ctx = ctx.shape[1]
    depth = ada_w.shape[0]
    sizes, offs, _ = _proj_layout(d)

    n_cond = bx + 1
    rows = -(-n_cond // 8) * 8
    cond = jnp.concatenate([c, c_ctx[None], jnp.zeros((rows - n_cond, d), F32)], axis=0)
    mods = adaln_all(cond, ada_w, ada_b)

    tabs = (_rope_tables(n_lat, HEAD_DIM // 2), _rope_tables(n_lat, HEAD_DIM // 4))

    def cols(p, name):
        lo = offs[name] * HEAD_DIM
        return p[:, :, lo:lo + sizes[name] * HEAD_DIM]

    xc = ctx
    pend = None
    pend_c = None
    for l in range(depth):
        last = l == depth - 1
        lam_init = 0.8 - 0.6 * math.exp(-0.3 * l)
        lat = [mods[l, :bx, None, i * d:(i + 1) * d] for i in range(N_MOD)]
        cm = [mods[l, bx:bx + 1, None, i * d:(i + 1) * d] for i in range(N_MOD)]
        w_in_l = w_in[l].astype(BF16)
        w_out_l = w_out[l].astype(BF16)
        lams = (lambda_q1[l], lambda_k1[l], lambda_q2[l], lambda_k2[l])

        outs = norm_stage(x, norm_mix[l], res=pend, mod=(lat[0], lat[1]))
        if pend is not None:
            x, u = outs
        else:
            (u,) = outs
        outs = norm_stage(xc, norm_mix[l], res=pend_c, mod=(cm[0], cm[1]))
        if pend_c is not None:
            xc, uc = outs
        else:
            (uc,) = outs

        p = in_projection(u, w_in_l, q_norm_c[l], k_norm_c[l], rope_tabs=tabs)
        pc = in_projection(uc, w_in_l, q_norm_c[l], k_norm_c[l])

        kb_all = jnp.concatenate([cols(p, "kb"), cols(pc, "kb")], axis=1)
        vb_all = jnp.concatenate([cols(p, "vb"), cols(pc, "vb")], axis=1)
        kc_all = jnp.concatenate([cols(p, "kc"), cols(pc, "kc")], axis=1)
        vc_all = jnp.concatenate([cols(p, "vc"), cols(pc, "vc")], axis=1)

        oa = attention_a(p, p, pc, sink_a[l], d, local=True)
        ob = attention_b(p, kb_all, vb_all, lams, subln_b[l], d, lam_init=lam_init)
        oc = attention_c(p, kc_all, vc_all, d)
        mix = jnp.concatenate([oa, ob, oc], axis=-1)
        x = out_projection(mix, w_out_l, x, lat[2])

        if not last:
            oa_c = attention_a(pc, pc, pc, sink_a[l], d, local=False)
            ob_c = attention_b(pc, cols(pc, "kb"), cols(pc, "vb"), lams, subln_b[l], d, lam_init=lam_init)
            oc_c = attention_c(pc, cols(pc, "kc"), cols(pc, "vc"), d)
            mix_c = jnp.concatenate([oa_c, ob_c, oc_c], axis=-1)
            xc = out_projection(mix_c, w_out_l, xc, cm[2])

        router = _router_weights(w_router_grp[l], b_router_grp[l], w_router_exp[l], b_router_exp[l])
        wg, wu, wd = w_gate[l].astype(BF16), w_up[l].astype(BF16), w_down[l].astype(BF16)
        t, comb = norm_stage(x, norm_ffn[l], mod=(lat[3], lat[4]), router=router)
        y = moe_experts(t.reshape(bx * n_lat, d), comb.reshape(bx * n_lat, ROUTER_LANES), wg, wu, wd)
        pend = (y.reshape(bx, n_lat, d), lat[5])
        if not last:
            tc, comb_c = norm_stage(xc, norm_ffn[l], mod=(cm[3], cm[4]), router=router)
            yc = moe_experts(tc.reshape(bx * n_ctx, d), comb_c.reshape(bx * n_ctx, ROUTER_LANES), wg, wu, wd)
            pend_c = (yc.reshape(bx, n_ctx, d), cm[5])

    _, out = norm_stage(x, final_norm, res=pend, out_dtype=F32)
    return out
```

```python
import functools
import math

import jax
import jax.numpy as jnp
from jax import lax
from jax.experimental import pallas as pl
from jax.experimental.pallas import tpu as pltpu

F32 = jnp.float32
BF16 = jnp.bfloat16

HEAD_DIM = 128
GRID_W = 64
WINDOW = 128
ROPE_THETA = 10000.0
ATT_SCALE = HEAD_DIM ** -0.5
DIFF_SCALE = (HEAD_DIM // 2) ** -0.5
LOG2E = math.log2(math.e)
N_GROUPS = 4
EXPERTS_PER_GROUP = 4
N_EXPERTS = N_GROUPS * EXPERTS_PER_GROUP
N_MOD = 6
NORM_EPS = 1e-6
NEG_INF = -1e30
ROUTER_LANES = 128
EXPERT_LANE0 = N_GROUPS

V7X_VMEM_LIMIT_BYTES = 52 * 1024 * 1024


def _cparams(*sem):
    return pltpu.CompilerParams(dimension_semantics=sem, vmem_limit_bytes=V7X_VMEM_LIMIT_BYTES)


def _pick(n, *cands):
    for c in cands:
        if n % c == 0:
            return c
    return n


def _adaln_body(cond_ref, w_ref, b_ref, o_ref):
    c = cond_ref[...]
    s = (c * jax.nn.sigmoid(c)).astype(BF16)
    o_ref[...] = jnp.dot(s, w_ref[...].astype(BF16), preferred_element_type=F32) + b_ref[...]


def adaln_all(cond, ada_w, ada_b):
    n_layers, d, n6 = ada_w.shape
    r = cond.shape[0]
    tn = _pick(n6, 512, 256, 128)
    return pl.pallas_call(
        _adaln_body,
        grid=(n_layers, n6 // tn),
        in_specs=[
            pl.BlockSpec((r, d), lambda l, j: (0, 0)),
            pl.BlockSpec((None, d, tn), lambda l, j: (l, 0, j)),
            pl.BlockSpec((None, 1, tn), lambda l, j: (l, 0, j)),
        ],
        out_specs=pl.BlockSpec((None, r, tn), lambda l, j: (l, 0, j)),
        out_shape=jax.ShapeDtypeStruct((n_layers, r, n6), F32),
        compiler_params=_cparams("arbitrary", "arbitrary"),
        name="adaln",
    )(cond, ada_w, ada_b.reshape(n_layers, 1, n6))


def _norm_body(*refs, has_res, has_mod, has_router):
    it = iter(refs)
    x_ref = next(it)
    if has_res:
        y_ref, gate_ref = next(it), next(it)
    g_ref = next(it)
    if has_mod:
        sh_ref, sc_ref = next(it), next(it)
    if has_router:
        wh_ref, wl_ref, rb_ref = next(it), next(it), next(it)
    if has_res:
        xo_ref = next(it)
    u_ref = next(it)
    if has_router:
        gid_ref = next(it)

    x = x_ref[...]
    if has_res:
        x = x + gate_ref[...] * y_ref[...].astype(F32)
        xo_ref[...] = x
    xn = x * lax.rsqrt(jnp.mean(x * x, axis=-1, keepdims=True) + NORM_EPS) * g_ref[...]
    if has_mod:
        xn = xn * (1.0 + sc_ref[...]) + sh_ref[...]
    if not has_router:
        u_ref[...] = xn.astype(u_ref.dtype)
    else:
        d = xn.shape[-1]
        u_ref[:, :d] = xn
        th = xn.astype(BF16)
        tl = (xn - th.astype(F32)).astype(BF16)
        wh = wh_ref[...]
        logits = (jnp.dot(th, wh, preferred_element_type=F32)
                  + jnp.dot(tl, wh, preferred_element_type=F32)
                  + jnp.dot(th, wl_ref[...], preferred_element_type=F32)) + rb_ref[...]
        lane = lax.broadcasted_iota(jnp.int32, logits.shape, 1)
        big = jnp.int32(ROUTER_LANES)
        lg = jnp.where(lane < N_GROUPS, logits, -jnp.inf)
        gmax = jnp.max(lg, axis=-1, keepdims=True)
        p_grp = 1.0 / jnp.sum(jnp.exp(lg - gmax), axis=-1, keepdims=True)
        gidx = jnp.min(jnp.where(lg == gmax, lane, big), axis=-1, keepdims=True)
        e_lo = EXPERT_LANE0 + EXPERTS_PER_GROUP * gidx
        le = jnp.where((lane >= e_lo) & (lane < e_lo + EXPERTS_PER_GROUP), logits, -jnp.inf)
        l1 = jnp.max(le, axis=-1, keepdims=True)
        i1 = jnp.min(jnp.where(le == l1, lane, big), axis=-1, keepdims=True)
        le2 = jnp.where(lane == i1, -jnp.inf, le)
        l2 = jnp.max(le2, axis=-1, keepdims=True)
        i2 = jnp.min(jnp.where(le2 == l2, lane, big), axis=-1, keepdims=True)
        e2 = jnp.exp(l2 - l1)
        w1 = p_grp / (1.0 + e2)
        w2 = w1 * e2
        comb = jnp.where(lane == i1, w1, jnp.where(lane == i2, w2, 0.0))
        u_ref[:, d:] = comb
        gid_ref[...] = jnp.broadcast_to(gidx, gid_ref.shape)


def norm_stage(x, gain, *, res=None, mod=None, router=None, out_dtype=BF16):
    b, t, d = x.shape
    ts = _pick(t, 256, 128, 64, 32, 16, 8)
    row = pl.BlockSpec((None, ts, d), lambda bi, i: (bi, i, 0))

    def per_batch(arr):
        if arr.shape[0] == b:
            return pl.BlockSpec((None, 1, d), lambda bi, i: (bi, 0, 0))
        return pl.BlockSpec((None, 1, d), lambda bi, i: (0, 0, 0))

    args, specs = [x], [row]
    if res is not None:
        y, gate = res
        args += [y, gate]
        specs += [row, per_batch(gate)]
    args.append(gain.reshape(1, d))
    specs.append(pl.BlockSpec((1, d), lambda bi, i: (0, 0)))
    if mod is not None:
        args += [mod[0], mod[1]]
        specs += [per_batch(mod[0]), per_batch(mod[1])]
    if router is not None:
        args += list(router)
        specs += [pl.BlockSpec((d, ROUTER_LANES), lambda bi, i: (0, 0)),
                  pl.BlockSpec((d, ROUTER_LANES), lambda bi, i: (0, 0)),
                  pl.BlockSpec((1, ROUTER_LANES), lambda bi, i: (0, 0))]
    out_shapes, out_specs = [], []
    if res is not None:
        out_shapes.append(jax.ShapeDtypeStruct((b, t, d), F32))
        out_specs.append(row)
    if router is None:
        out_shapes.append(jax.ShapeDtypeStruct((b, t, d), out_dtype))
        out_specs.append(row)
    else:
        out_shapes.append(jax.ShapeDtypeStruct((b, t, d + ROUTER_LANES), F32))
        out_specs.append(pl.BlockSpec((None, ts, d + ROUTER_LANES), lambda bi, i: (bi, i, 0)))
        out_shapes.append(jax.ShapeDtypeStruct((b, t, ROUTER_LANES), jnp.int32))
        out_specs.append(pl.BlockSpec((None, ts, ROUTER_LANES), lambda bi, i: (bi, i, 0)))
    body = functools.partial(_norm_body, has_res=res is not None, has_mod=mod is not None,
                             has_router=router is not None)
    return pl.pallas_call(
        body, grid=(b, t // ts), in_specs=specs, out_specs=out_specs, out_shape=out_shapes,
        compiler_params=_cparams("arbitrary", "arbitrary"), name="norm_stage",
    )(*args)


def _rope_tables(seq, block):
    half = block // 2
    lane = jnp.arange(HEAD_DIM)
    within = lane % block
    freq = (within % half).astype(F32)
    inv = ROPE_THETA ** (-(2.0 * freq) / block)
    t = jnp.arange(seq)
    pos = jnp.where(((lane // block) % 2 == 0)[None, :], (t // GRID_W)[:, None], (t % GRID_W)[:, None])
    ang = pos.astype(F32) * inv[None, :]
    cos, sin = jnp.cos(ang), jnp.sin(ang)
    first = (within < half)[None, :]
    return jnp.stack([cos, jnp.where(first, -sin, 0.0), jnp.where(first, 0.0, sin)])


def _inproj_body(*refs, segs, rope, heads_per_tile):
    if rope:
        a_ref, w_ref, gq_ref, gk_ref, tab_ac_ref, tab_b_ref, o_ref = refs
    else:
        a_ref, w_ref, gq_ref, gk_ref, o_ref = refs
    acc = jnp.dot(a_ref[...], w_ref[...], preferred_element_type=F32)
    j = pl.program_id(1)

    def emit(kind):
        for h in range(heads_per_tile):
            sl = slice(h * HEAD_DIM, (h + 1) * HEAD_DIM)
            y = acc[:, sl]
            if kind in ("qc", "kc"):
                g = gq_ref[...] if kind == "qc" else gk_ref[...]
                y = y * lax.rsqrt(jnp.mean(y * y, axis=-1, keepdims=True) + NORM_EPS) * g
            if rope and kind != "v":
                if kind in ("qb", "kb"):
                    tab, half = tab_b_ref, HEAD_DIM // 8
                else:
                    tab, half = tab_ac_ref, HEAD_DIM // 4
                y = (y * tab[0] + pltpu.roll(y, HEAD_DIM - half, 1) * tab[1]
                     + pltpu.roll(y, half, 1) * tab[2])
            if kind in ("qa", "qc"):
                y = y * (ATT_SCALE * LOG2E)
            elif kind == "qb":
                y = y * (DIFF_SCALE * LOG2E)
            o_ref[:, sl] = y.astype(o_ref.dtype)

    for lo, hi, kind in segs:
        pl.when((j >= lo) & (j < hi))(functools.partial(emit, kind))


def _proj_layout(d):
    h = d // HEAD_DIM
    a_h = 3 * h // 8
    a_kv = a_h // 3
    b_h = h // 4
    c_h = h - a_h - b_h
    c_kv = c_h // 3
    sizes = [a_h, a_kv, a_kv, b_h, b_h, b_h, c_h, c_kv, c_kv]
    names = ["qa", "ka", "va", "qb", "kb", "vb", "qc", "kc", "vc"]
    offs, acc = {}, 0
    for n, s in zip(names, sizes):
        offs[n] = acc
        acc += s
    return dict(zip(names, sizes)), offs, acc


def in_projection(u, w_bf16, q_gain, k_gain, *, rope_tabs=None):
    b, t, d = u.shape
    pw = w_bf16.shape[1]
    sizes, offs, total = _proj_layout(d)
    assert total * HEAD_DIM == pw
    m = b * t
    tm = _pick(t, 1024, 512, 256, 128) if rope_tabs is not None else _pick(m, 1024, 512, 256, 128)
    g = functools.reduce(math.gcd, sizes.values())
    hpt = _pick(g, 4, 2, 1)
    tn = hpt * HEAD_DIM
    kinds = {"qa": "qa", "ka": "ka", "va": "v", "qb": "qb", "kb": "kb", "vb": "v",
             "qc": "qc", "kc": "kc", "vc": "v"}
    segs = tuple((offs[n] // hpt, (offs[n] + sizes[n]) // hpt, kinds[n]) for n in sizes)
    in_specs = [
        pl.BlockSpec((tm, d), lambda i, j: (i, 0)),
        pl.BlockSpec((d, tn), lambda i, j: (0, j)),
        pl.BlockSpec((1, HEAD_DIM), lambda i, j: (0, 0)),
        pl.BlockSpec((1, HEAD_DIM), lambda i, j: (0, 0)),
    ]
    args = [u.reshape(m, d), w_bf16, q_gain.reshape(1, HEAD_DIM), k_gain.reshape(1, HEAD_DIM)]
    if rope_tabs is not None:
        nt = t // tm
        tab_spec = pl.BlockSpec((3, tm, HEAD_DIM), lambda i, j: (0, i % nt, 0))
        in_specs += [tab_spec, tab_spec]
        args += list(rope_tabs)
    body = functools.partial(_inproj_body, segs=segs, rope=rope_tabs is not None, heads_per_tile=hpt)
    out = pl.pallas_call(
        body, grid=(m // tm, pw // tn), in_specs=in_specs,
        out_specs=pl.BlockSpec((tm, tn), lambda i, j: (i, j)),
        out_shape=jax.ShapeDtypeStruct((m, pw), BF16),
        compiler_params=_cparams("arbitrary", "arbitrary"), name="in_projection",
    )(*args)
    return out.reshape(b, t, pw)


def _attn_a_body(sink_ref, q0_ref, q1_ref, q2_ref, kl_ref, vl_ref, kc_ref, vc_ref, o_ref, *, local, tq, wlen):
    hkv = pl.program_id(1)
    qi = pl.program_id(2)
    q = jnp.concatenate([q0_ref[...], q1_ref[...], q2_ref[...]], axis=0)
    nt = (((1,), (1,)), ((), ()))
    s_ctx = lax.dot_general(q, kc_ref[...], nt, preferred_element_type=F32)
    row = lax.broadcasted_iota(jnp.int32, (3 * tq, 1), 0)
    sink = LOG2E * jnp.where(row < tq, sink_ref[3 * hkv],
                             jnp.where(row < 2 * tq, sink_ref[3 * hkv + 1], sink_ref[3 * hkv + 2]))
    m = jnp.maximum(jnp.max(s_ctx, axis=-1, keepdims=True), sink)
    if local:
        n_lat = kl_ref.shape[0]
        start = pl.multiple_of(jnp.clip(qi * tq - WINDOW, 0, n_lat - wlen), WINDOW)
        k_win = kl_ref[pl.ds(start, wlen), :]
        v_win = vl_ref[pl.ds(start, wlen), :]
        s_loc = lax.dot_general(q, k_win, nt, preferred_element_type=F32)
        kpos = start + lax.broadcasted_iota(jnp.int32, (1, wlen), 1)
        qpos = qi * tq + jnp.where(row < tq, row, jnp.where(row < 2 * tq, row - tq, row - 2 * tq))
        s_loc = jnp.where(jnp.abs(kpos - qpos) <= WINDOW, s_loc, NEG_INF)
        m = jnp.maximum(m, jnp.max(s_loc, axis=-1, keepdims=True))
    p_ctx = jnp.exp2(s_ctx - m)
    l = jnp.sum(p_ctx, axis=-1, keepdims=True) + jnp.exp2(sink - m)
    o = jnp.dot(p_ctx.astype(BF16), vc_ref[...], preferred_element_type=F32)
    if local:
        p_loc = jnp.exp2(s_loc - m)
        l = l + jnp.sum(p_loc, axis=-1, keepdims=True)
        o = o + jnp.dot(p_loc.astype(BF16), v_win, preferred_element_type=F32)
    o = o / l
    for g in range(3):
        o_ref[:, g * HEAD_DIM:(g + 1) * HEAD_DIM] = o[g * tq:(g + 1) * tq].astype(o_ref.dtype)


def attention_a(p_q, p_lat, p_ctx, sink, d, *, local):
    sizes, offs, _ = _proj_layout(d)
    b, t, _ = p_q.shape
    n_ctx = p_ctx.shape[1]
    n_lat = p_lat.shape[1]
    tq = _pick(t, 256, 128)
    wlen = tq + 2 * WINDOW
    if local:
        assert n_lat >= wlen
    grid = (b, sizes["ka"], t // tq)

    def qspec(g):
        return pl.BlockSpec((None, tq, HEAD_DIM), lambda bi, h, i: (bi, i, offs["qa"] + 3 * h + g))

    def kvspec(n, name):
        return pl.BlockSpec((None, n, HEAD_DIM), lambda bi, h, i: (bi, 0, offs[name] + h))

    body = functools.partial(_attn_a_body, local=local, tq=tq, wlen=wlen)
    return pl.pallas_call(
        body, grid=grid,
        in_specs=[pl.BlockSpec(memory_space=pltpu.SMEM), qspec(0), qspec(1), qspec(2),
                  kvspec(n_lat, "ka"), kvspec(n_lat, "va"), kvspec(n_ctx, "ka"), kvspec(n_ctx, "va")],
        out_specs=pl.BlockSpec((None, tq, 3 * HEAD_DIM), lambda bi, h, i: (bi, i, h)),
        out_shape=jax.ShapeDtypeStruct((b, t, sizes["qa"] * HEAD_DIM), BF16),
        compiler_params=_cparams("arbitrary", "arbitrary", "arbitrary"), name="attention_a",
    )(sink, p_q, p_q, p_q, p_lat, p_lat, p_ctx, p_ctx)


def _flash_body(*refs, kind, has_lat, tq, kv_chunk, lam_init):
    it = iter(refs)
    if kind == "b":
        lq1_ref, lk1_ref, lq2_ref, lk2_ref, gain_ref = (next(it) for _ in range(5))
        q = next(it)[...]
        lane = lax.broadcasted_iota(jnp.int32, q.shape, 1)
        zero = jnp.zeros_like(q)
        q = jnp.concatenate([jnp.where(lane < HEAD_DIM // 2, q, zero),
                             jnp.where(lane < HEAD_DIM // 2, zero, q)], axis=0)
    else:
        q = jnp.concatenate([next(it)[...], next(it)[...], next(it)[...]], axis=0)
    if has_lat:
        kl_ref, vl_ref = next(it), next(it)
    kc_ref, vc_ref, o_ref, vaug_ref = next(it), next(it), next(it), next(it)

    n_lat = kl_ref.shape[0] if has_lat else 0
    n_ctx = kc_ref.shape[0]

    @pl.when(pl.program_id(2) == 0)
    def _():
        if has_lat:
            vaug_ref[0:n_lat, 0:HEAD_DIM] = vl_ref[...]
        vaug_ref[n_lat:n_lat + n_ctx, 0:HEAD_DIM] = vc_ref[...]
        vaug_ref[:, HEAD_DIM:] = jnp.ones((n_lat + n_ctx, HEAD_DIM), BF16)

    chunks = [(kl_ref, lo, lo, min(kv_chunk, n_lat - lo)) for lo in range(0, n_lat, kv_chunk)]
    chunks.append((kc_ref, 0, n_lat, n_ctx))

    rows = q.shape[0]
    m = jnp.full((rows, 1), -jnp.inf, F32)
    acc = jnp.zeros((rows, 2 * HEAD_DIM), F32)
    for k_ref, lo, vlo, n in chunks:
        s = lax.dot_general(q, k_ref[lo:lo + n, :], (((1,), (1,)), ((), ())), preferred_element_type=F32)
        m_new = jnp.maximum(m, jnp.max(s, axis=-1, keepdims=True))
        p = jnp.exp2(s - m_new)
        acc = jnp.exp2(m - m_new) * acc + jnp.dot(p.astype(BF16), vaug_ref[vlo:vlo + n, :],
                                                 preferred_element_type=F32)
        m = m_new
    o = acc[:, :HEAD_DIM] / acc[:, HEAD_DIM:]
    if kind == "b":
        lam = (jnp.exp(jnp.sum(lq1_ref[...] * lk1_ref[...], axis=-1, keepdims=True))
               - jnp.exp(jnp.sum(lq2_ref[...] * lk2_ref[...], axis=-1, keepdims=True)) + lam_init)
        o = o[:tq] - lam * o[tq:]
        o = o * lax.rsqrt(jnp.mean(o * o, axis=-1, keepdims=True) + NORM_EPS) * gain_ref[...]
        o_ref[...] = (o * (1.0 - lam_init)).astype(o_ref.dtype)
    else:
        for g in range(3):
            o_ref[:, g * HEAD_DIM:(g + 1) * HEAD_DIM] = o[g * tq:(g + 1) * tq].astype(o_ref.dtype)


def flash_attention(kind, p_q, p_lat, p_ctx, d, *, lams=None, gain=None, lam_init=0.0, tq=512, kv_chunk=1024):
    sizes, offs, _ = _proj_layout(d)
    b, t, _ = p_q.shape
    tq = _pick(t, tq, 128)
    qn, kn, vn = "q" + kind, "k" + kind, "v" + kind
    n_kv = sizes[kn]
    args, specs = [], []
    if kind == "b":
        half = HEAD_DIM // 2
        args += [v.reshape(1, half) for v in lams] + [gain.reshape(1, HEAD_DIM)]
        specs += [pl.BlockSpec((1, half), lambda bi, h, i: (0, 0))] * 4
        specs += [pl.BlockSpec((1, HEAD_DIM), lambda bi, h, i: (0, 0))]
        args.append(p_q)
        specs.append(pl.BlockSpec((None, tq, HEAD_DIM), lambda bi, h, i: (bi, i, offs[qn] + h)))
        out_w = HEAD_DIM
    else:
        for g in range(3):
            args.append(p_q)
            specs.append(pl.BlockSpec((None, tq, HEAD_DIM),
                                      lambda bi, h, i, g=g: (bi, i, offs[qn] + 3 * h + g)))
        out_w = 3 * HEAD_DIM

    def kv(arr, name):
        return pl.BlockSpec((None, arr.shape[1], HEAD_DIM), lambda bi, h, i: (bi, 0, offs[name] + h))

    if p_lat is not None:
        args += [p_lat, p_lat]
        specs += [kv(p_lat, kn), kv(p_lat, vn)]
    args += [p_ctx, p_ctx]
    specs += [kv(p_ctx, kn), kv(p_ctx, vn)]
    n_keys = p_ctx.shape[1] + (p_lat.shape[1] if p_lat is not None else 0)
    body = functools.partial(_flash_body, kind=kind, has_lat=p_lat is not None, tq=tq,
                             kv_chunk=kv_chunk, lam_init=lam_init)
    return pl.pallas_call(
        body, grid=(b, n_kv, t // tq), in_specs=specs,
        out_specs=pl.BlockSpec((None, tq, out_w), lambda bi, h, i: (bi, i, h)),
        out_shape=jax.ShapeDtypeStruct((b, t, sizes[qn] * HEAD_DIM), BF16),
        scratch_shapes=[pltpu.VMEM((n_keys, 2 * HEAD_DIM), BF16)],
        compiler_params=_cparams("arbitrary", "arbitrary", "arbitrary"), name="flash_" + kind,
    )(*args)


def _outproj_body(a_ref, w_ref, x_ref, gate_ref, o_ref):
    acc = jnp.dot(a_ref[...], w_ref[...], preferred_element_type=F32)
    o_ref[...] = x_ref[...] + gate_ref[...] * acc


def out_projection(mix, w_bf16, x, gate):
    b, t, k = mix.shape
    d = w_bf16.shape[1]
    m = b * t
    per_batch = gate.shape[0] == b
    tm = _pick(t, 1024, 512, 256, 128) if per_batch else _pick(m, 1024, 512, 256, 128)
    tn = _pick(d, 512, 256, 128)
    nt = t // tm if per_batch else 1
    gate_map = (lambda i, j: (i // nt, 0, j)) if per_batch else (lambda i, j: (0, 0, j))
    out = pl.pallas_call(
        _outproj_body, grid=(m // tm, d // tn),
        in_specs=[pl.BlockSpec((tm, k), lambda i, j: (i, 0)),
                  pl.BlockSpec((k, tn), lambda i, j: (0, j)),
                  pl.BlockSpec((tm, tn), lambda i, j: (i, j)),
                  pl.BlockSpec((None, 1, tn), gate_map)],
        out_specs=pl.BlockSpec((tm, tn), lambda i, j: (i, j)),
        out_shape=jax.ShapeDtypeStruct((m, d), F32),
        compiler_params=_cparams("arbitrary", "arbitrary"), name="out_projection",
    )(mix.reshape(m, k), w_bf16, x.reshape(m, d), gate)
    return out.reshape(b, t, d)


def _moe_plan(gid, tm):
    n = gid.shape[0]
    n_tiles = n // tm + N_GROUPS - 1
    onehot = (gid[:, None] == jnp.arange(N_GROUPS, dtype=jnp.int32)[None, :]).astype(jnp.int32)
    cnt = jnp.sum(onehot, axis=0)
    tiles_g = (cnt + tm - 1) // tm
    tile_end = jnp.cumsum(tiles_g)
    tile_start = tile_end - tiles_g
    rank = jnp.take_along_axis(jnp.cumsum(onehot, axis=0), gid[:, None], axis=1)[:, 0] - 1
    slot = tile_start[gid] * tm + rank
    src = jnp.zeros((n_tiles * tm,), jnp.int32).at[slot].set(jnp.arange(n, dtype=jnp.int32))
    tiles = jnp.arange(n_tiles, dtype=jnp.int32)
    tile_grp = jnp.minimum(jnp.searchsorted(tile_end, tiles, side="right").astype(jnp.int32), N_GROUPS - 1)
    n_valid = jnp.clip(cnt[tile_grp] - (tiles - tile_start[tile_grp]) * tm, 0, tm).astype(jnp.int32)
    return src, tile_grp, n_valid


def _moe_body(src_ref, grp_ref, nv_ref, t_hbm, wg_ref, wu_ref, wd_ref, y_hbm,
              tbuf, acc_ref, gsem, ssem, *, tm, d):
    tile = pl.program_id(0)
    e = pl.program_id(1)
    base = tile * tm

    def row_gather(r):
        return pltpu.make_async_copy(t_hbm.at[pl.ds(src_ref[base + r], 1)], tbuf.at[pl.ds(r, 1)], gsem)

    def row_scatter(r):
        return pltpu.make_async_copy(acc_ref.at[pl.ds(r, 1)], y_hbm.at[pl.ds(src_ref[base + r], 1)], ssem)

    def for_rows(n, fn, **kw):
        def step(r, carry):
            fn(r)
            return carry
        lax.fori_loop(0, n, step, 0, **kw)

    n_valid = nv_ref[tile]

    @pl.when(n_valid > 0)
    def _():
        @pl.when(e == 0)
        def _():
            for_rows(tm, lambda r: row_gather(r).start(), unroll=8)
            for_rows(tm, lambda r: row_gather(r).wait(), unroll=8)
            acc_ref[...] = jnp.zeros_like(acc_ref)

        t = tbuf[:, :d].astype(BF16)
        g = jnp.dot(t, wg_ref[...], preferred_element_type=F32)
        u = jnp.dot(t, wu_ref[...], preferred_element_type=F32)
        comb = tbuf[:, d:]
        lane = lax.broadcasted_iota(jnp.int32, comb.shape, 1)
        e_lane = EXPERT_LANE0 + EXPERTS_PER_GROUP * grp_ref[tile] + e
        cw = jnp.sum(jnp.where(lane == e_lane, comb, 0.0), axis=-1, keepdims=True)
        h = (g * jax.nn.sigmoid(g)) * u * cw
        acc_ref[...] += jnp.dot(h.astype(BF16), wd_ref[...], preferred_element_type=F32)

        @pl.when(e == EXPERTS_PER_GROUP - 1)
        def _():
            for_rows(n_valid, lambda r: row_scatter(r).start())
            for_rows(n_valid, lambda r: row_scatter(r).wait())


def moe_experts(t_ext, gid, wg, wu, wd):
    m, dw = t_ext.shape
    d = dw - ROUTER_LANES
    ff = wg.shape[2]
    tm = _pick(m, 512, 256, 128)
    src, tile_grp, n_valid = _moe_plan(gid, tm)
    n_tiles = tile_grp.shape[0]

    def w_spec(shape):
        return pl.BlockSpec((None,) + shape,
                            lambda i, e, src, grp, nv: (grp[i] * EXPERTS_PER_GROUP + e, 0, 0))

    grid_spec = pltpu.PrefetchScalarGridSpec(
        num_scalar_prefetch=3, grid=(n_tiles, EXPERTS_PER_GROUP),
        in_specs=[pl.BlockSpec(memory_space=pl.ANY), w_spec((d, ff)), w_spec((d, ff)), w_spec((ff, d))],
        out_specs=pl.BlockSpec(memory_space=pl.ANY),
        scratch_shapes=[pltpu.VMEM((tm, dw), F32), pltpu.VMEM((tm, d), F32),
                        pltpu.SemaphoreType.DMA(()), pltpu.SemaphoreType.DMA(())])
    return pl.pallas_call(
        functools.partial(_moe_body, tm=tm, d=d), grid_spec=grid_spec,
        out_shape=jax.ShapeDtypeStruct((m, d), F32),
        compiler_params=_cparams("arbitrary", "arbitrary"), name="moe_experts",
    )(src, tile_grp, n_valid, t_ext, wg, wu, wd)


def _router_weights(w_rg, b_rg, w_re, b_re):
    d = w_rg.shape[0]
    n_used = N_GROUPS + N_EXPERTS
    w = jnp.concatenate([w_rg, jnp.transpose(w_re, (1, 0, 2)).reshape(d, N_EXPERTS),
                         jnp.zeros((d, ROUTER_LANES - n_used), F32)], axis=1)
    bias = jnp.concatenate([b_rg, b_re.reshape(N_EXPERTS), jnp.zeros((ROUTER_LANES - n_used,), F32)])
    w_hi = w.astype(BF16)
    w_lo = (w - w_hi.astype(F32)).astype(BF16)
    return w_hi, w_lo, bias.reshape(1, ROUTER_LANES)


def kernel(x, c, ctx, c_ctx, ada_w, ada_b, norm_mix, w_in, sink_a, lambda_q1, lambda_k1, lambda_q2,
           lambda_k2, subln_b, q_norm_c, k_norm_c, w_out, norm_ffn, w_router_grp, b_router_grp,
           w_router_exp, b_router_exp, w_gate, w_up, w_down, final_norm):
    bx, n_lat, d = x.shape
    n_ctx = ctx.shape[1]
    depth = ada_w.shape[0]
    sizes, offs, _ = _proj_layout(d)

    n_cond = bx + 1
    rows = -(-n_cond // 8) * 8
    cond = jnp.concatenate([c, c_ctx[None], jnp.zeros((rows - n_cond, d), F32)], axis=0)
    mods = adaln_all(cond, ada_w, ada_b)

    tabs = (_rope_tables(n_lat, HEAD_DIM // 2), _rope_tables(n_lat, HEAD_DIM // 4))

    xc = ctx
    pend = None
    pend_c = None
    for l in range(depth):
        last = l == depth - 1
        lam_init = 0.8 - 0.6 * math.exp(-0.3 * l)
        lat = [mods[l, :bx, None, i * d:(i + 1) * d] for i in range(N_MOD)]
        cm = [mods[l, bx:bx + 1, None, i * d:(i + 1) * d] for i in range(N_MOD)]
        w_in_l = w_in[l].astype(BF16)
        w_out_l = w_out[l].astype(BF16)
        lams = (lambda_q1[l], lambda_k1[l], lambda_q2[l], lambda_k2[l])

        outs = norm_stage(x, norm_mix[l], res=pend, mod=(lat[0], lat[1]))
        if pend is not None:
            x, u = outs
        else:
            (u,) = outs
        outs = norm_stage(xc, norm_mix[l], res=pend_c, mod=(cm[0], cm[1]))
        if pend_c is not None:
            xc, uc = outs
        else:
            (uc,) = outs

        p = in_projection(u, w_in_l, q_norm_c[l], k_norm_c[l], rope_tabs=tabs)
        pc = in_projection(uc, w_in_l, q_norm_c[l], k_norm_c[l])

        oa = attention_a(p, p, pc, sink_a[l], d, local=True)
        ob = flash_attention("b", p, p, pc, d, lams=lams, gain=subln_b[l], lam_init=lam_init)
        oc = flash_attention("c", p, p, pc, d)
        mix = jnp.concatenate([oa, ob, oc], axis=-1)
        x = out_projection(mix, w_out_l, x, lat[2])

        if not last:
            oa_c = attention_a(pc, pc, pc, sink_a[l], d, local=False)
            ob_c = flash_attention("b", pc, None, pc, d, lams=lams, gain=subln_b[l], lam_init=lam_init)
            oc_c = flash_attention("c", pc, None, pc, d)
            mix_c = jnp.concatenate([oa_c, ob_c, oc_c], axis=-1)
            xc = out_projection(mix_c, w_out_l, xc, cm[2])

        router = _router_weights(w_router_grp[l], b_router_grp[l], w_router_exp[l], b_router_exp[l])
        wg, wu, wd = w_gate[l].astype(BF16), w_up[l].astype(BF16), w_down[l].astype(BF16)
        t, gid = norm_stage(x, norm_ffn[l], mod=(lat[3], lat[4]), router=router)
        y = moe_experts(t.reshape(bx * n_lat, d + ROUTER_LANES), gid[:, :, 0].reshape(bx * n_lat), wg, wu, wd)
        pend = (y.reshape(bx, n_lat, d), lat[5])
        if not last:
            tc, gid_c = norm_stage(xc, norm_ffn[l], mod=(cm[3], cm[4]), router=router)
            yc = moe_experts(tc.reshape(bx * n_ctx, d + ROUTER_LANES), gid_c[:, :, 0].reshape(bx * n_ctx),
                             wg, wu, wd)
            pend_c = (yc.reshape(bx, n_ctx, d), cm[5])

    _, out = norm_stage(x, final_norm, res=pend, out_dtype=F32)
    return out
```

```python
import functools
import math

import jax
import jax.numpy as jnp
from jax import lax
from jax.experimental import pallas as pl
from jax.experimental.pallas import tpu as pltpu

F32 = jnp.float32
BF16 = jnp.bfloat16

HEAD_DIM = 128
GRID_W = 64
WINDOW = 128
ROPE_THETA = 10000.0
ATT_SCALE = HEAD_DIM ** -0.5
DIFF_SCALE = (HEAD_DIM // 2) ** -0.5
LOG2E = math.log2(math.e)
N_GROUPS = 4
EXPERTS_PER_GROUP = 4
N_EXPERTS = N_GROUPS * EXPERTS_PER_GROUP
N_MOD = 6
NORM_EPS = 1e-6
NEG_INF = -1e30
ROUTER_LANES = 128
EXPERT_LANE0 = N_GROUPS

V7X_VMEM_LIMIT_BYTES = 52 * 1024 * 1024


def _cparams(*sem):
    return pltpu.CompilerParams(dimension_semantics=sem, vmem_limit_bytes=V7X_VMEM_LIMIT_BYTES)


def _pick(n, *cands):
    for c in cands:
        if n % c == 0:
            return c
    return n


def _adaln_body(cond_ref, w_ref, b_ref, o_ref):
    c = cond_ref[...]
    s = (c * jax.nn.sigmoid(c)).astype(BF16)
    o_ref[...] = jnp.dot(s, w_ref[...].astype(BF16), preferred_element_type=F32) + b_ref[...]


def adaln_all(cond, ada_w, ada_b):
    n_layers, d, n6 = ada_w.shape
    r = cond.shape[0]
    tn = _pick(n6, 512, 256, 128)
    return pl.pallas_call(
        _adaln_body,
        grid=(n_layers, n6 // tn),
        in_specs=[
            pl.BlockSpec((r, d), lambda l, j: (0, 0)),
            pl.BlockSpec((None, d, tn), lambda l, j: (l, 0, j)),
            pl.BlockSpec((None, 1, tn), lambda l, j: (l, 0, j)),
        ],
        out_specs=pl.BlockSpec((None, r, tn), lambda l, j: (l, 0, j)),
        out_shape=jax.ShapeDtypeStruct((n_layers, r, n6), F32),
        compiler_params=_cparams("arbitrary", "arbitrary"),
        name="adaln",
    )(cond, ada_w, ada_b.reshape(n_layers, 1, n6))


def _norm_body(*refs, has_res, keep_x, has_mod, has_router):
    it = iter(refs)
    x_ref = next(it)
    if has_res:
        y_ref, gate_ref = next(it), next(it)
    g_ref = next(it)
    if has_mod:
        sh_ref, sc_ref = next(it), next(it)
    if has_router:
        wh_ref, wl_ref, rb_ref = next(it), next(it), next(it)
    if keep_x:
        xo_ref = next(it)
    u_ref = next(it)
    if has_router:
        gid_ref = next(it)

    x = x_ref[...]
    if has_res:
        x = x + gate_ref[...] * y_ref[...].astype(F32)
    if keep_x:
        xo_ref[...] = x
    xn = x * lax.rsqrt(jnp.mean(x * x, axis=-1, keepdims=True) + NORM_EPS) * g_ref[...]
    if has_mod:
        xn = xn * (1.0 + sc_ref[...]) + sh_ref[...]
    if not has_router:
        u_ref[...] = xn.astype(u_ref.dtype)
    else:
        d = xn.shape[-1]
        u_ref[:, :d] = xn
        th = xn.astype(BF16)
        tl = (xn - th.astype(F32)).astype(BF16)
        wh = wh_ref[...]
        logits = (jnp.dot(th, wh, preferred_element_type=F32)
                  + jnp.dot(tl, wh, preferred_element_type=F32)
                  + jnp.dot(th, wl_ref[...], preferred_element_type=F32)) + rb_ref[...]
        lane = lax.broadcasted_iota(jnp.int32, logits.shape, 1)
        big = jnp.int32(ROUTER_LANES)
        lg = jnp.where(lane < N_GROUPS, logits, -jnp.inf)
        gmax = jnp.max(lg, axis=-1, keepdims=True)
        p_grp = 1.0 / jnp.sum(jnp.exp(lg - gmax), axis=-1, keepdims=True)
        gidx = jnp.min(jnp.where(lg == gmax, lane, big), axis=-1, keepdims=True)
        e_lo = EXPERT_LANE0 + EXPERTS_PER_GROUP * gidx
        le = jnp.where((lane >= e_lo) & (lane < e_lo + EXPERTS_PER_GROUP), logits, -jnp.inf)
        l1 = jnp.max(le, axis=-1, keepdims=True)
        i1 = jnp.min(jnp.where(le == l1, lane, big), axis=-1, keepdims=True)
        le2 = jnp.where(lane == i1, -jnp.inf, le)
        l2 = jnp.max(le2, axis=-1, keepdims=True)
        i2 = jnp.min(jnp.where(le2 == l2, lane, big), axis=-1, keepdims=True)
        e2 = jnp.exp(l2 - l1)
        w1 = p_grp / (1.0 + e2)
        w2 = w1 * e2
        comb = jnp.where(lane == i1, w1, jnp.where(lane == i2, w2, 0.0))
        u_ref[:, d:] = comb
        gid_ref[...] = jnp.broadcast_to(gidx, gid_ref.shape)


def norm_stage(x, gain, *, res=None, keep_x=None, mod=None, router=None, out_dtype=BF16):
    keep_x = (res is not None) if keep_x is None else keep_x
    b, t, d = x.shape
    ts = _pick(t, 256, 128, 64, 32, 16, 8)
    row = pl.BlockSpec((None, ts, d), lambda bi, i: (bi, i, 0))

    def per_batch(arr):
        if arr.shape[0] == b:
            return pl.BlockSpec((None, 1, d), lambda bi, i: (bi, 0, 0))
        return pl.BlockSpec((None, 1, d), lambda bi, i: (0, 0, 0))

    args, specs = [x], [row]
    if res is not None:
        y, gate = res
        args += [y, gate]
        specs += [row, per_batch(gate)]
    args.append(gain.reshape(1, d))
    specs.append(pl.BlockSpec((1, d), lambda bi, i: (0, 0)))
    if mod is not None:
        args += [mod[0], mod[1]]
        specs += [per_batch(mod[0]), per_batch(mod[1])]
    if router is not None:
        args += list(router)
        specs += [pl.BlockSpec((d, ROUTER_LANES), lambda bi, i: (0, 0)),
                  pl.BlockSpec((d, ROUTER_LANES), lambda bi, i: (0, 0)),
                  pl.BlockSpec((1, ROUTER_LANES), lambda bi, i: (0, 0))]
    out_shapes, out_specs = [], []
    if keep_x:
        out_shapes.append(jax.ShapeDtypeStruct((b, t, d), F32))
        out_specs.append(row)
    if router is None:
        out_shapes.append(jax.ShapeDtypeStruct((b, t, d), out_dtype))
        out_specs.append(row)
    else:
        out_shapes.append(jax.ShapeDtypeStruct((b, t, d + ROUTER_LANES), F32))
        out_specs.append(pl.BlockSpec((None, ts, d + ROUTER_LANES), lambda bi, i: (bi, i, 0)))
        out_shapes.append(jax.ShapeDtypeStruct((b, t, ROUTER_LANES), jnp.int32))
        out_specs.append(pl.BlockSpec((None, ts, ROUTER_LANES), lambda bi, i: (bi, i, 0)))
    body = functools.partial(_norm_body, has_res=res is not None, keep_x=keep_x, has_mod=mod is not None,
                             has_router=router is not None)
    return pl.pallas_call(
        body, grid=(b, t // ts), in_specs=specs, out_specs=out_specs, out_shape=out_shapes,
        compiler_params=_cparams("arbitrary", "arbitrary"), name="norm_stage",
    )(*args)


_PERM_AC = ((2, 2, HEAD_DIM // 4), (1, 0, 2))
_PERM_B = ((2, 2, 2, HEAD_DIM // 8), (2, 0, 1, 3))


def _permute_head_lanes(arr, perm):
    split, order = perm
    lead = arr.shape[:-1]
    nl = len(lead)
    out = arr.reshape(lead + split).transpose(tuple(range(nl)) + tuple(nl + o for o in order))
    return out.reshape(lead + (HEAD_DIM,))


def _permute_proj_columns(w, d):
    sizes, offs, _ = _proj_layout(d)
    pieces = []
    for name in sizes:
        lo, n = offs[name] * HEAD_DIM, sizes[name]
        blk = w[:, lo:lo + n * HEAD_DIM]
        if name[0] != "v":
            perm = _PERM_B if name[1] == "b" else _PERM_AC
            blk = _permute_head_lanes(blk.reshape(w.shape[0], n, HEAD_DIM), perm).reshape(w.shape[0], n * HEAD_DIM)
        pieces.append(blk)
    return jnp.concatenate(pieces, axis=1)


def _rope_tables(seq, group_b):
    lane = jnp.arange(HEAD_DIM)
    if group_b:
        axis, f, block = (lane % 32) // 16, lane % 16, HEAD_DIM // 4
    else:
        axis, f, block = (lane % 64) // 32, lane % 32, HEAD_DIM // 2
    inv = ROPE_THETA ** (-(2.0 * f.astype(F32)) / block)
    t = jnp.arange(seq)
    pos = jnp.where((axis == 0)[None, :], (t // GRID_W)[:, None], (t % GRID_W)[:, None])
    ang = pos.astype(F32) * inv[None, :]
    return jnp.stack([jnp.cos(ang), jnp.where((lane < HEAD_DIM // 2)[None, :], -1.0, 1.0) * jnp.sin(ang)])


def _inproj_body(*refs, segs, rope, heads_per_tile):
    if rope:
        a_ref, w_ref, gq_ref, gk_ref, tab_ac_ref, tab_b_ref, o_ref, acc_ref = refs
    else:
        a_ref, w_ref, gq_ref, gk_ref, o_ref, acc_ref = refs
    j = pl.program_id(1)

    norm_lo = min(lo for lo, _, kind in segs if kind in ("qc", "kc"))
    norm_hi = max(hi for _, hi, kind in segs if kind in ("qc", "kc"))

    @pl.when((j >= norm_lo) & (j < norm_hi))
    def _():
        acc_ref[...] = jnp.dot(a_ref[...], w_ref[...], preferred_element_type=F32)

    def emit(kind):
        normed = kind in ("qc", "kc")
        sub = heads_per_tile if normed else min(heads_per_tile, 2)
        for c in range(heads_per_tile // sub):
            c0 = c * sub * HEAD_DIM
            if not normed:
                acc = jnp.dot(a_ref[...], w_ref[:, c0:c0 + sub * HEAD_DIM], preferred_element_type=F32)
            for h in range(sub):
                y = acc_ref[:, h * HEAD_DIM:(h + 1) * HEAD_DIM] if normed else acc[:, h * HEAD_DIM:(h + 1) * HEAD_DIM]
                if kind in ("qc", "kc"):
                    g = gq_ref[...] if kind == "qc" else gk_ref[...]
                    y = y * lax.rsqrt(jnp.mean(y * y, axis=-1, keepdims=True) + NORM_EPS) * g
                if rope and kind != "v":
                    tab = tab_b_ref if kind in ("qb", "kb") else tab_ac_ref
                    y = y * tab[0] + pltpu.roll(y, HEAD_DIM // 2, 1) * tab[1]
                if kind in ("qa", "qc"):
                    y = y * (ATT_SCALE * LOG2E)
                elif kind == "qb":
                    y = y * (DIFF_SCALE * LOG2E)
                o_ref[:, c0 + h * HEAD_DIM:c0 + (h + 1) * HEAD_DIM] = y.astype(o_ref.dtype)

    for lo, hi, kind in segs:
        pl.when((j >= lo) & (j < hi))(functools.partial(emit, kind))


def _proj_layout(d):
    h = d // HEAD_DIM
    a_h = 3 * h // 8
    a_kv = a_h // 3
    b_h = h // 4
    c_h = h - a_h - b_h
    c_kv = c_h // 3
    sizes = [a_h, a_kv, a_kv, b_h, b_h, b_h, c_h, c_kv, c_kv]
    names = ["qa", "ka", "va", "qb", "kb", "vb", "qc", "kc", "vc"]
    offs, acc = {}, 0
    for n, s in zip(names, sizes):
        offs[n] = acc
        acc += s
    return dict(zip(names, sizes)), offs, acc


def in_projection(u, w_bf16, q_gain, k_gain, *, rope_tabs=None):
    b, t, d = u.shape
    pw = w_bf16.shape[1]
    sizes, offs, total = _proj_layout(d)
    assert total * HEAD_DIM == pw
    m = b * t
    tm = _pick(t, 1024, 512, 256, 128) if rope_tabs is not None else _pick(m, 1024, 512, 256, 128)
    g = functools.reduce(math.gcd, sizes.values())
    hpt = _pick(g, 4, 2, 1)
    tn = hpt * HEAD_DIM
    kinds = {"qa": "qa", "ka": "ka", "va": "v", "qb": "qb", "kb": "kb", "vb": "v",
             "qc": "qc", "kc": "kc", "vc": "v"}
    segs = tuple((offs[n] // hpt, (offs[n] + sizes[n]) // hpt, kinds[n]) for n in sizes)
    in_specs = [
        pl.BlockSpec((tm, d), lambda i, j: (i, 0)),
        pl.BlockSpec((d, tn), lambda i, j: (0, j)),
        pl.BlockSpec((1, HEAD_DIM), lambda i, j: (0, 0)),
        pl.BlockSpec((1, HEAD_DIM), lambda i, j: (0, 0)),
    ]
    args = [u.reshape(m, d), w_bf16, q_gain.reshape(1, HEAD_DIM), k_gain.reshape(1, HEAD_DIM)]
    if rope_tabs is not None:
        nt = t // tm
        tab_spec = pl.BlockSpec((2, tm, HEAD_DIM), lambda i, j: (0, i % nt, 0))
        in_specs += [tab_spec, tab_spec]
        args += list(rope_tabs)
    body = functools.partial(_inproj_body, segs=segs, rope=rope_tabs is not None, heads_per_tile=hpt)
    out = pl.pallas_call(
        body, grid=(m // tm, pw // tn), in_specs=in_specs,
        out_specs=pl.BlockSpec((tm, tn), lambda i, j: (i, j)),
        out_shape=jax.ShapeDtypeStruct((m, pw), BF16),
        scratch_shapes=[pltpu.VMEM((tm, tn), F32)],
        compiler_params=_cparams("arbitrary", "arbitrary"), name="in_projection",
    )(*args)
    return out.reshape(b, t, pw)


def _attn_a_body(sink_ref, q0_ref, q1_ref, q2_ref, kl_ref, vl_ref, kc_ref, vc_ref, o_ref, *, local, tq, wlen):
    hkv = pl.program_id(1)
    qi = pl.program_id(2)
    q = jnp.concatenate([q0_ref[...], q1_ref[...], q2_ref[...]], axis=0)
    nt = (((1,), (1,)), ((), ()))
    s_ctx = lax.dot_general(q, kc_ref[...], nt, preferred_element_type=F32)
    row = lax.broadcasted_iota(jnp.int32, (3 * tq, 1), 0)
    sink = LOG2E * jnp.where(row < tq, sink_ref[3 * hkv],
                             jnp.where(row < 2 * tq, sink_ref[3 * hkv + 1], sink_ref[3 * hkv + 2]))
    m = jnp.maximum(jnp.max(s_ctx, axis=-1, keepdims=True), sink)
    if local:
        n_lat = kl_ref.shape[0]
        start = pl.multiple_of(jnp.clip(qi * tq - WINDOW, 0, n_lat - wlen), WINDOW)
        k_win = kl_ref[pl.ds(start, wlen), :]
        v_win = vl_ref[pl.ds(start, wlen), :]
        s_loc = lax.dot_general(q, k_win, nt, preferred_element_type=F32)
        kpos = start + lax.broadcasted_iota(jnp.int32, (1, wlen), 1)
        qpos = qi * tq + jnp.where(row < tq, row, jnp.where(row < 2 * tq, row - tq, row - 2 * tq))
        s_loc = jnp.where(jnp.abs(kpos - qpos) <= WINDOW, s_loc, NEG_INF)
        m = jnp.maximum(m, jnp.max(s_loc, axis=-1, keepdims=True))
    p_ctx = jnp.exp2(s_ctx - m)
    l = jnp.sum(p_ctx, axis=-1, keepdims=True) + jnp.exp2(sink - m)
    o = jnp.dot(p_ctx.astype(BF16), vc_ref[...], preferred_element_type=F32)
    if local:
        p_loc = jnp.exp2(s_loc - m)
        l = l + jnp.sum(p_loc, axis=-1, keepdims=True)
        o = o + jnp.dot(p_loc.astype(BF16), v_win, preferred_element_type=F32)
    o = o / l
    for g in range(3):
        o_ref[:, g * HEAD_DIM:(g + 1) * HEAD_DIM] = o[g * tq:(g + 1) * tq].astype(o_ref.dtype)


def attention_a(p_q, p_lat, p_ctx, sink, d, *, local):
    sizes, offs, _ = _proj_layout(d)
    b, t, _ = p_q.shape
    n_ctx = p_ctx.shape[1]
    n_lat = p_lat.shape[1]
    tq = _pick(t, 256, 128)
    wlen = tq + 2 * WINDOW
    if local:
        assert n_lat >= wlen
    grid = (b, sizes["ka"], t // tq)

    def qspec(g):
        return pl.BlockSpec((None, tq, HEAD_DIM), lambda bi, h, i: (bi, i, offs["qa"] + 3 * h + g))

    def kvspec(n, name):
        return pl.BlockSpec((None, n, HEAD_DIM), lambda bi, h, i: (bi, 0, offs[name] + h))

    body = functools.partial(_attn_a_body, local=local, tq=tq, wlen=wlen)
    return pl.pallas_call(
        body, grid=grid,
        in_specs=[pl.BlockSpec(memory_space=pltpu.SMEM), qspec(0), qspec(1), qspec(2),
                  kvspec(n_lat, "ka"), kvspec(n_lat, "va"), kvspec(n_ctx, "ka"), kvspec(n_ctx, "va")],
        out_specs=pl.BlockSpec((None, tq, 3 * HEAD_DIM), lambda bi, h, i: (bi, i, h)),
        out_shape=jax.ShapeDtypeStruct((b, t, sizes["qa"] * HEAD_DIM), BF16),
        compiler_params=_cparams("arbitrary", "arbitrary", "arbitrary"), name="attention_a",
    )(sink, p_q, p_q, p_q, p_lat, p_lat, p_ctx, p_ctx)


def _flash_body(*refs, kind, has_lat, tq, kv_chunk, lam_init):
    it = iter(refs)
    if kind == "b":
        lq1_ref, lk1_ref, lq2_ref, lk2_ref, gain_ref = (next(it) for _ in range(5))
        q = next(it)[...]
        lane = lax.broadcasted_iota(jnp.int32, q.shape, 1)
        sub1 = (lane % (HEAD_DIM // 2)) < HEAD_DIM // 4
        zero = jnp.zeros_like(q)
        q = jnp.concatenate([jnp.where(sub1, q, zero), jnp.where(sub1, zero, q)], axis=0)
    else:
        q = jnp.concatenate([next(it)[...], next(it)[...], next(it)[...]], axis=0)
    if has_lat:
        kl_ref, vl_ref = next(it), next(it)
    kc_ref, vc_ref, o_ref, vaug_ref = next(it), next(it), next(it), next(it)

    n_lat = kl_ref.shape[0] if has_lat else 0
    n_ctx = kc_ref.shape[0]

    @pl.when(pl.program_id(2) == 0)
    def _():
        if has_lat:
            vaug_ref[0:n_lat, 0:HEAD_DIM] = vl_ref[...]
        vaug_ref[n_lat:n_lat + n_ctx, 0:HEAD_DIM] = vc_ref[...]
        vaug_ref[:, HEAD_DIM:] = jnp.ones((n_lat + n_ctx, HEAD_DIM), BF16)

    chunks = [(kl_ref, lo, lo, min(kv_chunk, n_lat - lo)) for lo in range(0, n_lat, kv_chunk)]
    chunks.append((kc_ref, 0, n_lat, n_ctx))

    rows = q.shape[0]
    m = jnp.full((rows, 1), -jnp.inf, F32)
    acc = jnp.zeros((rows, 2 * HEAD_DIM), F32)
    for k_ref, lo, vlo, n in chunks:
        s = lax.dot_general(q, k_ref[lo:lo + n, :], (((1,), (1,)), ((), ())), preferred_element_type=F32)
        m_new = jnp.maximum(m, jnp.max(s, axis=-1, keepdims=True))
        p = jnp.exp2(s - m_new)
        acc = jnp.exp2(m - m_new) * acc + jnp.dot(p.astype(BF16), vaug_ref[vlo:vlo + n, :],
                                                 preferred_element_type=F32)
        m = m_new
    o = acc[:, :HEAD_DIM] / acc[:, HEAD_DIM:]
    if kind == "b":
        lam = (jnp.exp(jnp.sum(lq1_ref[...] * lk1_ref[...], axis=-1, keepdims=True))
               - jnp.exp(jnp.sum(lq2_ref[...] * lk2_ref[...], axis=-1, keepdims=True)) + lam_init)
        o = o[:tq] - lam * o[tq:]
        o = o * lax.rsqrt(jnp.mean(o * o, axis=-1, keepdims=True) + NORM_EPS) * gain_ref[...]
        o_ref[...] = (o * (1.0 - lam_init)).astype(o_ref.dtype)
    else:
        for g in range(3):
            o_ref[:, g * HEAD_DIM:(g + 1) * HEAD_DIM] = o[g * tq:(g + 1) * tq].astype(o_ref.dtype)


def flash_attention(kind, p_q, p_lat, p_ctx, d, *, lams=None, gain=None, lam_init=0.0, tq=512, kv_chunk=1024):
    sizes, offs, _ = _proj_layout(d)
    b, t, _ = p_q.shape
    tq = _pick(t, tq, 128)
    qn, kn, vn = "q" + kind, "k" + kind, "v" + kind
    n_kv = sizes[kn]
    args, specs = [], []
    if kind == "b":
        half = HEAD_DIM // 2
        args += [v.reshape(1, half) for v in lams] + [gain.reshape(1, HEAD_DIM)]
        specs += [pl.BlockSpec((1, half), lambda bi, h, i: (0, 0))] * 4
        specs += [pl.BlockSpec((1, HEAD_DIM), lambda bi, h, i: (0, 0))]
        args.append(p_q)
        specs.append(pl.BlockSpec((None, tq, HEAD_DIM), lambda bi, h, i: (bi, i, offs[qn] + h)))
        out_w = HEAD_DIM
    else:
        for g in range(3):
            args.append(p_q)
            specs.append(pl.BlockSpec((None, tq, HEAD_DIM),
                                      lambda bi, h, i, g=g: (bi, i, offs[qn] + 3 * h + g)))
        out_w = 3 * HEAD_DIM

    def kv(arr, name):
        return pl.BlockSpec((None, arr.shape[1], HEAD_DIM), lambda bi, h, i: (bi, 0, offs[name] + h))

    if p_lat is not None:
        args += [p_lat, p_lat]
        specs += [kv(p_lat, kn), kv(p_lat, vn)]
    args += [p_ctx, p_ctx]
    specs += [kv(p_ctx, kn), kv(p_ctx, vn)]
    n_keys = p_ctx.shape[1] + (p_lat.shape[1] if p_lat is not None else 0)
    body = functools.partial(_flash_body, kind=kind, has_lat=p_lat is not None, tq=tq,
                             kv_chunk=kv_chunk, lam_init=lam_init)
    return pl.pallas_call(
        body, grid=(b, n_kv, t // tq), in_specs=specs,
        out_specs=pl.BlockSpec((None, tq, out_w), lambda bi, h, i: (bi, i, h)),
        out_shape=jax.ShapeDtypeStruct((b, t, sizes[qn] * HEAD_DIM), BF16),
        scratch_shapes=[pltpu.VMEM((n_keys, 2 * HEAD_DIM), BF16)],
        compiler_params=_cparams("arbitrary", "arbitrary", "arbitrary"), name="flash_" + kind,
    )(*args)


def _outproj_body(*refs, n_parts):
    a_refs, w_refs = refs[:n_parts], refs[n_parts:2 * n_parts]
    x_ref, gate_ref, o_ref = refs[2 * n_parts:]
    acc = jnp.dot(a_refs[0][...], w_refs[0][...], preferred_element_type=F32)
    for a_ref, w_ref in zip(a_refs[1:], w_refs[1:]):
        acc += jnp.dot(a_ref[...], w_ref[...], preferred_element_type=F32)
    o_ref[...] = x_ref[...] + gate_ref[...] * acc


def out_projection(parts, w_bf16, x, gate):
    b, t, d = x.shape
    m = b * t
    per_batch = gate.shape[0] == b
    tm = _pick(t, 1024, 512, 256, 128) if per_batch else _pick(m, 1024, 512, 256, 128)
    tn = _pick(d, 512, 256, 128)
    nt = t // tm if per_batch else 1
    gate_map = (lambda i, j: (i // nt, 0, j)) if per_batch else (lambda i, j: (0, 0, j))
    widths = [p.shape[-1] for p in parts]
    w_parts, lo = [], 0
    for k in widths:
        w_parts.append(w_bf16[lo:lo + k])
        lo += k
    assert lo == w_bf16.shape[0]
    out = pl.pallas_call(
        functools.partial(_outproj_body, n_parts=len(parts)), grid=(m // tm, d // tn),
        in_specs=([pl.BlockSpec((tm, k), lambda i, j: (i, 0)) for k in widths]
                  + [pl.BlockSpec((k, tn), lambda i, j: (0, j)) for k in widths]
                  + [pl.BlockSpec((tm, tn), lambda i, j: (i, j)),
                     pl.BlockSpec((None, 1, tn), gate_map)]),
        out_specs=pl.BlockSpec((tm, tn), lambda i, j: (i, j)),
        out_shape=jax.ShapeDtypeStruct((m, d), F32),
        compiler_params=_cparams("arbitrary", "arbitrary"), name="out_projection",
    )(*[p.reshape(m, k) for p, k in zip(parts, widths)], *w_parts, x.reshape(m, d), gate)
    return out.reshape(b, t, d)


def _moe_plan(gid, tm):
    n = gid.shape[0]
    n_tiles = n // tm + N_GROUPS - 1
    onehot = (gid[:, None] == jnp.arange(N_GROUPS, dtype=jnp.int32)[None, :]).astype(jnp.int32)
    cnt = jnp.sum(onehot, axis=0)
    tiles_g = (cnt + tm - 1) // tm
    tile_end = jnp.cumsum(tiles_g)
    tile_start = tile_end - tiles_g
    rank = jnp.take_along_axis(jnp.cumsum(onehot, axis=0), gid[:, None], axis=1)[:, 0] - 1
    slot = tile_start[gid] * tm + rank
    src = jnp.zeros((n_tiles * tm,), jnp.int32).at[slot].set(jnp.arange(n, dtype=jnp.int32))
    tiles = jnp.arange(n_tiles, dtype=jnp.int32)
    tile_grp = jnp.minimum(jnp.sum((tile_end[None, :] <= tiles[:, None]).astype(jnp.int32), axis=1),
                           N_GROUPS - 1)
    n_valid = jnp.clip(cnt[tile_grp] - (tiles - tile_start[tile_grp]) * tm, 0, tm).astype(jnp.int32)
    return src, tile_grp, n_valid


def _moe_body(src_ref, grp_ref, nv_ref, t_hbm, wg_ref, wu_ref, wd_ref, y_hbm,
              tbuf, acc_ref, gsem, ssem, *, tm, d, ff, fc):
    tile = pl.program_id(0)
    chunk = pl.program_id(1)
    n_tiles = pl.num_programs(0)
    n_chunks = pl.num_programs(1)
    slot = tile % 2

    def row_gather(tl, sl, r):
        return pltpu.make_async_copy(t_hbm.at[pl.ds(src_ref[tl * tm + r], 1)],
                                     tbuf.at[sl, pl.ds(r, 1)], gsem.at[sl])

    def row_scatter(tl, r):
        return pltpu.make_async_copy(acc_ref.at[pl.ds(r, 1)],
                                     y_hbm.at[pl.ds(src_ref[tl * tm + r], 1)], ssem)

    def for_rows(n, fn, **kw):
        def step(r, carry):
            fn(r)
            return carry
        lax.fori_loop(0, n, step, 0, **kw)

    def active(tl):
        return nv_ref[jnp.clip(tl, 0, n_tiles - 1)] > 0

    next_active = (tile + 1 < n_tiles) & active(tile + 1)

    @pl.when(active(tile))
    def _():
        @pl.when(chunk == 0)
        def _():
            @pl.when(tile == 0)
            def _():
                for_rows(tm, lambda r: row_gather(0, 0, r).start(), unroll=8)

            @pl.when(next_active)
            def _():
                for_rows(tm, lambda r: row_gather(tile + 1, 1 - slot, r).start(), unroll=8)

            @pl.when(tile > 0)
            def _():
                for_rows(nv_ref[jnp.maximum(tile - 1, 0)], lambda r: row_scatter(tile - 1, r).wait())

            acc_ref[...] = jnp.zeros_like(acc_ref)
            for_rows(tm, lambda r: row_gather(tile, slot, r).wait(), unroll=8)

        t = tbuf[slot, :, :d].astype(BF16)
        g = jnp.dot(t, wg_ref[...], preferred_element_type=F32)
        u = jnp.dot(t, wu_ref[...], preferred_element_type=F32)
        comb = tbuf[slot, :, d:]
        lane = lax.broadcasted_iota(jnp.int32, comb.shape, 1)
        lane0 = EXPERT_LANE0 + EXPERTS_PER_GROUP * grp_ref[tile]
        cw = [jnp.sum(jnp.where(lane == lane0 + e, comb, 0.0), axis=-1, keepdims=True)
              for e in range(EXPERTS_PER_GROUP)]
        col = chunk * fc + lax.broadcasted_iota(jnp.int32, (1, fc), 1)
        cw_col = cw[EXPERTS_PER_GROUP - 1]
        for e in range(EXPERTS_PER_GROUP - 2, -1, -1):
            cw_col = jnp.where(col < (e + 1) * ff, cw[e], cw_col)
        h = (g * jax.nn.sigmoid(g)) * u * cw_col
        acc_ref[...] += jnp.dot(h.astype(BF16), wd_ref[...], preferred_element_type=F32)

        @pl.when(chunk == n_chunks - 1)
        def _():
            n_valid = nv_ref[tile]

            @pl.when(n_valid == tm)
            def _():
                for_rows(tm, lambda r: row_scatter(tile, r).start(), unroll=8)

            @pl.when(n_valid < tm)
            def _():
                for_rows(n_valid, lambda r: row_scatter(tile, r).start())

            @pl.when(jnp.logical_not(next_active))
            def _():
                for_rows(n_valid, lambda r: row_scatter(tile, r).wait())


def moe_experts(t_ext, gid, wg_grp, wu_grp, wd_grp):
    m, dw = t_ext.shape
    d = dw - ROUTER_LANES
    gff = wg_grp.shape[2]
    ff = gff // EXPERTS_PER_GROUP
    fc = _pick(gff, 256, 128)
    tm = 512 if m % 512 == 0 and m >= 8192 else _pick(m, 256, 128)
    src, tile_grp, n_valid = _moe_plan(gid, tm)
    n_tiles = tile_grp.shape[0]

    grid_spec = pltpu.PrefetchScalarGridSpec(
        num_scalar_prefetch=3, grid=(n_tiles, gff // fc),
        in_specs=[pl.BlockSpec(memory_space=pl.ANY),
                  pl.BlockSpec((None, d, fc), lambda i, c, src, grp, nv: (grp[i], 0, c)),
                  pl.BlockSpec((None, d, fc), lambda i, c, src, grp, nv: (grp[i], 0, c)),
                  pl.BlockSpec((None, fc, d), lambda i, c, src, grp, nv: (grp[i], c, 0))],
        out_specs=pl.BlockSpec(memory_space=pl.ANY),
        scratch_shapes=[pltpu.VMEM((2, tm, dw), F32), pltpu.VMEM((tm, d), F32),
                        pltpu.SemaphoreType.DMA((2,)), pltpu.SemaphoreType.DMA(())])
    return pl.pallas_call(
        functools.partial(_moe_body, tm=tm, d=d, ff=ff, fc=fc), grid_spec=grid_spec,
        out_shape=jax.ShapeDtypeStruct((m, d), F32),
        compiler_params=_cparams("arbitrary", "arbitrary"), name="moe_experts",
    )(src, tile_grp, n_valid, t_ext, wg_grp, wu_grp, wd_grp)


def _group_expert_weights(w_gate, w_up, w_down):
    n_e, d, ff = w_gate.shape

    def cat_cols(w):
        w = w.reshape(N_GROUPS, EXPERTS_PER_GROUP, d, ff).transpose(0, 2, 1, 3)
        return w.reshape(N_GROUPS, d, EXPERTS_PER_GROUP * ff).astype(BF16)

    return cat_cols(w_gate), cat_cols(w_up), w_down.reshape(N_GROUPS, EXPERTS_PER_GROUP * ff, d).astype(BF16)


def _router_weights(w_rg, b_rg, w_re, b_re):
    d = w_rg.shape[0]
    n_used = N_GROUPS + N_EXPERTS
    w = jnp.concatenate([w_rg, jnp.transpose(w_re, (1, 0, 2)).reshape(d, N_EXPERTS),
                         jnp.zeros((d, ROUTER_LANES - n_used), F32)], axis=1)
    bias = jnp.concatenate([b_rg, b_re.reshape(N_EXPERTS), jnp.zeros((ROUTER_LANES - n_used,), F32)])
    w_hi = w.astype(BF16)
    w_lo = (w - w_hi.astype(F32)).astype(BF16)
    return w_hi, w_lo, bias.reshape(1, ROUTER_LANES)


def kernel(x, c, ctx, c_ctx, ada_w, ada_b, norm_mix, w_in, sink_a, lambda_q1, lambda_k1, lambda_q2,
           lambda_k2, subln_b, q_norm_c, k_norm_c, w_out, norm_ffn, w_router_grp, b_router_grp,
           w_router_exp, b_router_exp, w_gate, w_up, w_down, final_norm):
    bx, n_lat, d = x.shape
    n_ctx = ctx.shape[1]
    depth = ada_w.shape[0]
    sizes, offs, _ = _proj_layout(d)

    n_cond = bx + 1
    rows = -(-n_cond // 8) * 8
    cond = jnp.concatenate([c, c_ctx[None], jnp.zeros((rows - n_cond, d), F32)], axis=0)
    mods = adaln_all(cond, ada_w, ada_b)

    tabs = (_rope_tables(n_lat, False), _rope_tables(n_lat, True))

    xc = ctx
    pend = None
    pend_c = None
    for l in range(depth):
        last = l == depth - 1
        lam_init = 0.8 - 0.6 * math.exp(-0.3 * l)
        lat = [mods[l, :bx, None, i * d:(i + 1) * d] for i in range(N_MOD)]
        cm = [mods[l, bx:bx + 1, None, i * d:(i + 1) * d] for i in range(N_MOD)]
        w_in_l = _permute_proj_columns(w_in[l], d).astype(BF16)
        gq = _permute_head_lanes(q_norm_c[l], _PERM_AC)
        gk = _permute_head_lanes(k_norm_c[l], _PERM_AC)
        w_out_l = w_out[l].astype(BF16)
        lams = (lambda_q1[l], lambda_k1[l], lambda_q2[l], lambda_k2[l])

        outs = norm_stage(x, norm_mix[l], res=pend, mod=(lat[0], lat[1]))
        if pend is not None:
            x, u = outs
        else:
            (u,) = outs
        outs = norm_stage(xc, norm_mix[l], res=pend_c, mod=(cm[0], cm[1]))
        if pend_c is not None:
            xc, uc = outs
        else:
            (uc,) = outs

        p = in_projection(u, w_in_l, gq, gk, rope_tabs=tabs)
        pc = in_projection(uc, w_in_l, gq, gk)

        oa = attention_a(p, p, pc, sink_a[l], d, local=True)
        ob = flash_attention("b", p, p, pc, d, lams=lams, gain=subln_b[l], lam_init=lam_init)
        oc = flash_attention("c", p, p, pc, d)
        x = out_projection((oa, ob, oc), w_out_l, x, lat[2])

        if not last:
            oa_c = attention_a(pc, pc, pc, sink_a[l], d, local=False)
            ob_c = flash_attention("b", pc, None, pc, d, lams=lams, gain=subln_b[l], lam_init=lam_init)
            oc_c = flash_attention("c", pc, None, pc, d)
            xc = out_projection((oa_c, ob_c, oc_c), w_out_l, xc, cm[2])

        router = _router_weights(w_router_grp[l], b_router_grp[l], w_router_exp[l], b_router_exp[l])
        wg, wu, wd = _group_expert_weights(w_gate[l], w_up[l], w_down[l])
        t, gid = norm_stage(x, norm_ffn[l], mod=(lat[3], lat[4]), router=router)
        y = moe_experts(t.reshape(bx * n_lat, d + ROUTER_LANES), gid[:, :, 0].reshape(bx * n_lat), wg, wu, wd)
        pend = (y.reshape(bx, n_lat, d), lat[5])
        if not last:
            tc, gid_c = norm_stage(xc, norm_ffn[l], mod=(cm[3], cm[4]), router=router)
            yc = moe_experts(tc.reshape(bx * n_ctx, d + ROUTER_LANES), gid_c[:, :, 0].reshape(bx * n_ctx),
                             wg, wu, wd)
            pend_c = (yc.reshape(bx, n_ctx, d), cm[5])

    (out,) = norm_stage(x, final_norm, res=pend, keep_x=False, out_dtype=F32)
    return out
```

```python
import functools
import math

import jax
import jax.numpy as jnp
from jax import lax
from jax.experimental import pallas as pl
from jax.experimental.pallas import tpu as pltpu

F32 = jnp.float32
BF16 = jnp.bfloat16

HEAD_DIM = 128
GRID_W = 64
WINDOW = 128
ROPE_THETA = 10000.0
ATT_SCALE = HEAD_DIM ** -0.5
DIFF_SCALE = (HEAD_DIM // 2) ** -0.5
LOG2E = math.log2(math.e)
N_GROUPS = 4
EXPERTS_PER_GROUP = 4
N_EXPERTS = N_GROUPS * EXPERTS_PER_GROUP
N_MOD = 6
NORM_EPS = 1e-6
NEG_INF = -1e30
ROUTER_LANES = 128
EXPERT_LANE0 = N_GROUPS

V7X_VMEM_LIMIT_BYTES = 52 * 1024 * 1024


def _cparams(*sem):
    return pltpu.CompilerParams(dimension_semantics=sem, vmem_limit_bytes=V7X_VMEM_LIMIT_BYTES)


def _pick(n, *cands):
    for c in cands:
        if n % c == 0:
            return c
    return n


def _adaln_body(cond_ref, w_ref, b_ref, o_ref):
    c = cond_ref[...]
    s = (c * jax.nn.sigmoid(c)).astype(BF16)
    o_ref[...] = jnp.dot(s, w_ref[...].astype(BF16), preferred_element_type=F32) + b_ref[...]


def adaln_all(cond, ada_w, ada_b):
    n_layers, d, n6 = ada_w.shape
    r = cond.shape[0]
    tn = _pick(n6, 512, 256, 128)
    return pl.pallas_call(
        _adaln_body,
        grid=(n_layers, n6 // tn),
        in_specs=[
            pl.BlockSpec((r, d), lambda l, j: (0, 0)),
            pl.BlockSpec((None, d, tn), lambda l, j: (l, 0, j)),
            pl.BlockSpec((None, 1, tn), lambda l, j: (l, 0, j)),
        ],
        out_specs=pl.BlockSpec((None, r, tn), lambda l, j: (l, 0, j)),
        out_shape=jax.ShapeDtypeStruct((n_layers, r, n6), F32),
        compiler_params=_cparams("arbitrary", "arbitrary"),
        name="adaln",
    )(cond, ada_w, ada_b.reshape(n_layers, 1, n6))


def _norm_body(*refs, has_res, keep_x, has_mod, has_router):
    it = iter(refs)
    x_ref = next(it)
    if has_res:
        y_ref, gate_ref = next(it), next(it)
    g_ref = next(it)
    if has_mod:
        sh_ref, sc_ref = next(it), next(it)
    if has_router:
        whl_ref, rb_ref = next(it), next(it)
    if keep_x:
        xo_ref = next(it)
    u_ref = next(it)
    if has_router:
        gid_ref = next(it)

    x = x_ref[...]
    if has_res:
        x = x + gate_ref[...] * y_ref[...].astype(F32)
    if keep_x:
        xo_ref[...] = x
    xn = x * lax.rsqrt(jnp.mean(x * x, axis=-1, keepdims=True) + NORM_EPS) * g_ref[...]
    if has_mod:
        xn = xn * (1.0 + sc_ref[...]) + sh_ref[...]
    if not has_router:
        u_ref[...] = xn.astype(u_ref.dtype)
    else:
        d = xn.shape[-1]
        u_ref[:, :d] = xn
        th = xn.astype(BF16)
        tl = (xn - th.astype(F32)).astype(BF16)
        hi = jnp.dot(th, whl_ref[...], preferred_element_type=F32)
        logits = (hi[:, :ROUTER_LANES] + hi[:, ROUTER_LANES:]
                  + jnp.dot(tl, whl_ref[:, :ROUTER_LANES], preferred_element_type=F32)) + rb_ref[...]
        lane = lax.broadcasted_iota(jnp.int32, logits.shape, 1)
        big = jnp.int32(ROUTER_LANES)
        lg = jnp.where(lane < N_GROUPS, logits, -jnp.inf)
        gmax = jnp.max(lg, axis=-1, keepdims=True)
        p_grp = 1.0 / jnp.sum(jnp.exp(lg - gmax), axis=-1, keepdims=True)
        gidx = jnp.min(jnp.where(lg == gmax, lane, big), axis=-1, keepdims=True)
        e_lo = EXPERT_LANE0 + EXPERTS_PER_GROUP * gidx
        le = jnp.where((lane >= e_lo) & (lane < e_lo + EXPERTS_PER_GROUP), logits, -jnp.inf)
        l1 = jnp.max(le, axis=-1, keepdims=True)
        i1 = jnp.min(jnp.where(le == l1, lane, big), axis=-1, keepdims=True)
        le2 = jnp.where(lane == i1, -jnp.inf, le)
        l2 = jnp.max(le2, axis=-1, keepdims=True)
        i2 = jnp.min(jnp.where(le2 == l2, lane, big), axis=-1, keepdims=True)
        e2 = jnp.exp(l2 - l1)
        w1 = p_grp / (1.0 + e2)
        w2 = w1 * e2
        comb = jnp.where(lane == i1, w1, jnp.where(lane == i2, w2, 0.0))
        u_ref[:, d:] = comb
        gid_ref[...] = jnp.broadcast_to(gidx, gid_ref.shape)


def norm_stage(x, gain, *, res=None, keep_x=None, mod=None, router=None, out_dtype=BF16):
    keep_x = (res is not None) if keep_x is None else keep_x
    b, t, d = x.shape
    ts = _pick(t, 256, 128, 64, 32, 16, 8)
    row = pl.BlockSpec((None, ts, d), lambda bi, i: (bi, i, 0))

    def per_batch(arr):
        if arr.shape[0] == b:
            return pl.BlockSpec((None, 1, d), lambda bi, i: (bi, 0, 0))
        return pl.BlockSpec((None, 1, d), lambda bi, i: (0, 0, 0))

    args, specs = [x], [row]
    if res is not None:
        y, gate = res
        args += [y, gate]
        specs += [row, per_batch(gate)]
    args.append(gain.reshape(1, d))
    specs.append(pl.BlockSpec((1, d), lambda bi, i: (0, 0)))
    if mod is not None:
        args += [mod[0], mod[1]]
        specs += [per_batch(mod[0]), per_batch(mod[1])]
    if router is not None:
        args += list(router)
        specs += [pl.BlockSpec((d, 2 * ROUTER_LANES), lambda bi, i: (0, 0)),
                  pl.BlockSpec((1, ROUTER_LANES), lambda bi, i: (0, 0))]
    out_shapes, out_specs = [], []
    if keep_x:
        out_shapes.append(jax.ShapeDtypeStruct((b, t, d), F32))
        out_specs.append(row)
    if router is None:
        out_shapes.append(jax.ShapeDtypeStruct((b, t, d), out_dtype))
        out_specs.append(row)
    else:
        out_shapes.append(jax.ShapeDtypeStruct((b, t, d + ROUTER_LANES), F32))
        out_specs.append(pl.BlockSpec((None, ts, d + ROUTER_LANES), lambda bi, i: (bi, i, 0)))
        out_shapes.append(jax.ShapeDtypeStruct((b, t, ROUTER_LANES), jnp.int32))
        out_specs.append(pl.BlockSpec((None, ts, ROUTER_LANES), lambda bi, i: (bi, i, 0)))
    body = functools.partial(_norm_body, has_res=res is not None, keep_x=keep_x, has_mod=mod is not None,
                             has_router=router is not None)
    return pl.pallas_call(
        body, grid=(b, t // ts), in_specs=specs, out_specs=out_specs, out_shape=out_shapes,
        compiler_params=_cparams("arbitrary", "arbitrary"), name="norm_stage",
    )(*args)


_PERM_AC = ((2, 2, HEAD_DIM // 4), (1, 0, 2))
_PERM_B = ((2, 2, 2, HEAD_DIM // 8), (2, 0, 1, 3))


def _permute_head_lanes(arr, perm):
    split, order = perm
    lead = arr.shape[:-1]
    nl = len(lead)
    out = arr.reshape(lead + split).transpose(tuple(range(nl)) + tuple(nl + o for o in order))
    return out.reshape(lead + (HEAD_DIM,))


def _head_permutations(d):
    sizes, _, _ = _proj_layout(d)
    eye = jnp.eye(HEAD_DIM, dtype=BF16)
    mats = []
    for name, n in sizes.items():
        p = eye if name[0] == "v" else _permute_head_lanes(eye, _PERM_B if name[1] == "b" else _PERM_AC)
        mats.append(jnp.broadcast_to(p, (n, HEAD_DIM, HEAD_DIM)))
    return jnp.concatenate(mats, axis=0)


def _cast_permute_body(w_ref, p_ref, o_ref):
    w = w_ref[...].astype(BF16)
    for h in range(p_ref.shape[0]):
        sl = slice(h * HEAD_DIM, (h + 1) * HEAD_DIM)
        o_ref[:, sl] = jnp.dot(w[:, sl], p_ref[h], preferred_element_type=F32).astype(BF16)


def cast_permute_proj(w, perms):
    d, pw = w.shape
    tr = _pick(d, 1024, 512, 256, 128)
    hpt = _pick(pw // HEAD_DIM, 4, 2, 1)
    tn = hpt * HEAD_DIM
    return pl.pallas_call(
        _cast_permute_body, grid=(d // tr, pw // tn),
        in_specs=[pl.BlockSpec((tr, tn), lambda i, j: (i, j)),
                  pl.BlockSpec((hpt, HEAD_DIM, HEAD_DIM), lambda i, j: (j, 0, 0))],
        out_specs=pl.BlockSpec((tr, tn), lambda i, j: (i, j)),
        out_shape=jax.ShapeDtypeStruct((d, pw), BF16),
        compiler_params=_cparams("arbitrary", "arbitrary"), name="cast_permute_proj",
    )(w, perms)


def _rope_tables(seq, group_b):
    lane = jnp.arange(HEAD_DIM)
    if group_b:
        axis, f, block = (lane % 32) // 16, lane % 16, HEAD_DIM // 4
    else:
        axis, f, block = (lane % 64) // 32, lane % 32, HEAD_DIM // 2
    inv = ROPE_THETA ** (-(2.0 * f.astype(F32)) / block)
    t = jnp.arange(seq)
    pos = jnp.where((axis == 0)[None, :], (t // GRID_W)[:, None], (t % GRID_W)[:, None])
    ang = pos.astype(F32) * inv[None, :]
    return jnp.stack([jnp.cos(ang), jnp.where((lane < HEAD_DIM // 2)[None, :], -1.0, 1.0) * jnp.sin(ang)])


def _inproj_body(*refs, segs, rope, heads_per_tile):
    if rope:
        a_ref, w_ref, gq_ref, gk_ref, tab_ac_ref, tab_b_ref, o_ref, acc_ref = refs
    else:
        a_ref, w_ref, gq_ref, gk_ref, o_ref, acc_ref = refs
    j = pl.program_id(1)

    norm_lo = min(lo for lo, _, kind in segs if kind in ("qc", "kc"))
    norm_hi = max(hi for _, hi, kind in segs if kind in ("qc", "kc"))

    @pl.when((j >= norm_lo) & (j < norm_hi))
    def _():
        acc_ref[...] = jnp.dot(a_ref[...], w_ref[...], preferred_element_type=F32)

    def emit(kind):
        normed = kind in ("qc", "kc")
        sub = heads_per_tile if normed else min(heads_per_tile, 2)
        for c in range(heads_per_tile // sub):
            c0 = c * sub * HEAD_DIM
            if not normed:
                acc = jnp.dot(a_ref[...], w_ref[:, c0:c0 + sub * HEAD_DIM], preferred_element_type=F32)
            for h in range(sub):
                y = acc_ref[:, h * HEAD_DIM:(h + 1) * HEAD_DIM] if normed else acc[:, h * HEAD_DIM:(h + 1) * HEAD_DIM]
                if kind in ("qc", "kc"):
                    g = gq_ref[...] if kind == "qc" else gk_ref[...]
                    y = y * lax.rsqrt(jnp.mean(y * y, axis=-1, keepdims=True) + NORM_EPS) * g
                if rope and kind != "v":
                    tab = tab_b_ref if kind in ("qb", "kb") else tab_ac_ref
                    y = y * tab[0] + pltpu.roll(y, HEAD_DIM // 2, 1) * tab[1]
                if kind in ("qa", "qc"):
                    y = y * (ATT_SCALE * LOG2E)
                elif kind == "qb":
                    y = y * (DIFF_SCALE * LOG2E)
                o_ref[:, c0 + h * HEAD_DIM:c0 + (h + 1) * HEAD_DIM] = y.astype(o_ref.dtype)

    for lo, hi, kind in segs:
        pl.when((j >= lo) & (j < hi))(functools.partial(emit, kind))


def _proj_layout(d):
    h = d // HEAD_DIM
    a_h = 3 * h // 8
    a_kv = a_h // 3
    b_h = h // 4
    c_h = h - a_h - b_h
    c_kv = c_h // 3
    sizes = [a_h, a_kv, a_kv, b_h, b_h, b_h, c_h, c_kv, c_kv]
    names = ["qa", "ka", "va", "qb", "kb", "vb", "qc", "kc", "vc"]
    offs, acc = {}, 0
    for n, s in zip(names, sizes):
        offs[n] = acc
        acc += s
    return dict(zip(names, sizes)), offs, acc


def in_projection(u, w_bf16, q_gain, k_gain, *, rope_tabs=None):
    b, t, d = u.shape
    pw = w_bf16.shape[1]
    sizes, offs, total = _proj_layout(d)
    assert total * HEAD_DIM == pw
    m = b * t
    tm = _pick(t, 1024, 512, 256, 128) if rope_tabs is not None else _pick(m, 1024, 512, 256, 128)
    g = functools.reduce(math.gcd, sizes.values())
    hpt = _pick(g, 4, 2, 1)
    tn = hpt * HEAD_DIM
    kinds = {"qa": "qa", "ka": "ka", "va": "v", "qb": "qb", "kb": "kb", "vb": "v",
             "qc": "qc", "kc": "kc", "vc": "v"}
    segs = tuple((offs[n] // hpt, (offs[n] + sizes[n]) // hpt, kinds[n]) for n in sizes)
    in_specs = [
        pl.BlockSpec((tm, d), lambda i, j: (i, 0)),
        pl.BlockSpec((d, tn), lambda i, j: (0, j)),
        pl.BlockSpec((1, HEAD_DIM), lambda i, j: (0, 0)),
        pl.BlockSpec((1, HEAD_DIM), lambda i, j: (0, 0)),
    ]
    args = [u.reshape(m, d), w_bf16, q_gain.reshape(1, HEAD_DIM), k_gain.reshape(1, HEAD_DIM)]
    if rope_tabs is not None:
        nt = t // tm
        tab_spec = pl.BlockSpec((2, tm, HEAD_DIM), lambda i, j: (0, i % nt, 0))
        in_specs += [tab_spec, tab_spec]
        args += list(rope_tabs)
    body = functools.partial(_inproj_body, segs=segs, rope=rope_tabs is not None, heads_per_tile=hpt)
    out = pl.pallas_call(
        body, grid=(m // tm, pw // tn), in_specs=in_specs,
        out_specs=pl.BlockSpec((tm, tn), lambda i, j: (i, j)),
        out_shape=jax.ShapeDtypeStruct((m, pw), BF16),
        scratch_shapes=[pltpu.VMEM((tm, tn), F32)],
        compiler_params=_cparams("arbitrary", "arbitrary"), name="in_projection",
    )(*args)
    return out.reshape(b, t, pw)


def _attn_a_body(sink_ref, q0_ref, q1_ref, q2_ref, kl_ref, vl_ref, kc_ref, vc_ref, o_ref, *, local, tq, wlen):
    hkv = pl.program_id(1)
    qi = pl.program_id(2)
    q = jnp.concatenate([q0_ref[...], q1_ref[...], q2_ref[...]], axis=0)
    nt = (((1,), (1,)), ((), ()))
    s_ctx = lax.dot_general(q, kc_ref[...], nt, preferred_element_type=F32)
    row = lax.broadcasted_iota(jnp.int32, (3 * tq, 1), 0)
    sink = LOG2E * jnp.where(row < tq, sink_ref[3 * hkv],
                             jnp.where(row < 2 * tq, sink_ref[3 * hkv + 1], sink_ref[3 * hkv + 2]))
    m = jnp.maximum(jnp.max(s_ctx, axis=-1, keepdims=True), sink)
    if local:
        n_lat = kl_ref.shape[0]
        start = pl.multiple_of(jnp.clip(qi * tq - WINDOW, 0, n_lat - wlen), WINDOW)
        k_win = kl_ref[pl.ds(start, wlen), :]
        v_win = vl_ref[pl.ds(start, wlen), :]
        s_loc = lax.dot_general(q, k_win, nt, preferred_element_type=F32)
        kpos = start + lax.broadcasted_iota(jnp.int32, (1, wlen), 1)
        qpos = qi * tq + jnp.where(row < tq, row, jnp.where(row < 2 * tq, row - tq, row - 2 * tq))
        s_loc = jnp.where(jnp.abs(kpos - qpos) <= WINDOW, s_loc, NEG_INF)
        m = jnp.maximum(m, jnp.max(s_loc, axis=-1, keepdims=True))
    p_ctx = jnp.exp2(s_ctx - m)
    l = jnp.sum(p_ctx, axis=-1, keepdims=True) + jnp.exp2(sink - m)
    o = jnp.dot(p_ctx.astype(BF16), vc_ref[...], preferred_element_type=F32)
    if local:
        p_loc = jnp.exp2(s_loc - m)
        l = l + jnp.sum(p_loc, axis=-1, keepdims=True)
        o = o + jnp.dot(p_loc.astype(BF16), v_win, preferred_element_type=F32)
    o = o / l
    for g in range(3):
        o_ref[:, g * HEAD_DIM:(g + 1) * HEAD_DIM] = o[g * tq:(g + 1) * tq].astype(o_ref.dtype)


def attention_a(p_q, p_lat, p_ctx, sink, d, *, local):
    sizes, offs, _ = _proj_layout(d)
    b, t, _ = p_q.shape
    n_ctx = p_ctx.shape[1]
    n_lat = p_lat.shape[1]
    tq = _pick(t, 256, 128)
    wlen = tq + 2 * WINDOW
    if local:
        assert n_lat >= wlen
    grid = (b, sizes["ka"], t // tq)

    def qspec(g):
        return pl.BlockSpec((None, tq, HEAD_DIM), lambda bi, h, i: (bi, i, offs["qa"] + 3 * h + g))

    def kvspec(n, name):
        return pl.BlockSpec((None, n, HEAD_DIM), lambda bi, h, i: (bi, 0, offs[name] + h))

    body = functools.partial(_attn_a_body, local=local, tq=tq, wlen=wlen)
    return pl.pallas_call(
        body, grid=grid,
        in_specs=[pl.BlockSpec(memory_space=pltpu.SMEM), qspec(0), qspec(1), qspec(2),
                  kvspec(n_lat, "ka"), kvspec(n_lat, "va"), kvspec(n_ctx, "ka"), kvspec(n_ctx, "va")],
        out_specs=pl.BlockSpec((None, tq, 3 * HEAD_DIM), lambda bi, h, i: (bi, i, h)),
        out_shape=jax.ShapeDtypeStruct((b, t, sizes["qa"] * HEAD_DIM), BF16),
        compiler_params=_cparams("arbitrary", "arbitrary", "arbitrary"), name="attention_a",
    )(sink, p_q, p_q, p_q, p_lat, p_lat, p_ctx, p_ctx)


def _flash_body(*refs, kind, has_lat, tq, kv_chunk, lam_init):
    it = iter(refs)
    if kind == "b":
        lq1_ref, lk1_ref, lq2_ref, lk2_ref, gain_ref = (next(it) for _ in range(5))
        q = next(it)[...]
        lane = lax.broadcasted_iota(jnp.int32, q.shape, 1)
        sub1 = (lane % (HEAD_DIM // 2)) < HEAD_DIM // 4
        zero = jnp.zeros_like(q)
        q = jnp.concatenate([jnp.where(sub1, q, zero), jnp.where(sub1, zero, q)], axis=0)
    else:
        q = jnp.concatenate([next(it)[...], next(it)[...], next(it)[...]], axis=0)
    if has_lat:
        kl_ref, vl_ref = next(it), next(it)
    kc_ref, vc_ref, o_ref, vaug_ref = next(it), next(it), next(it), next(it)

    n_lat = kl_ref.shape[0] if has_lat else 0
    n_ctx = kc_ref.shape[0]

    @pl.when(pl.program_id(2) == 0)
    def _():
        if has_lat:
            vaug_ref[0:n_lat, 0:HEAD_DIM] = vl_ref[...]
        vaug_ref[n_lat:n_lat + n_ctx, 0:HEAD_DIM] = vc_ref[...]
        vaug_ref[:, HEAD_DIM:] = jnp.ones((n_lat + n_ctx, HEAD_DIM), BF16)

    chunks = [(kl_ref, lo, lo, min(kv_chunk, n_lat - lo)) for lo in range(0, n_lat, kv_chunk)]
    chunks.append((kc_ref, 0, n_lat, n_ctx))

    rows = q.shape[0]
    m = jnp.full((rows, 1), -jnp.inf, F32)
    acc = jnp.zeros((rows, 2 * HEAD_DIM), F32)
    for k_ref, lo, vlo, n in chunks:
        s = lax.dot_general(q, k_ref[lo:lo + n, :], (((1,), (1,)), ((), ())), preferred_element_type=F32)
        m_new = jnp.maximum(m, jnp.max(s, axis=-1, keepdims=True))
        p = jnp.exp2(s - m_new)
        acc = jnp.exp2(m - m_new) * acc + jnp.dot(p.astype(BF16), vaug_ref[vlo:vlo + n, :],
                                                 preferred_element_type=F32)
        m = m_new
    o = acc[:, :HEAD_DIM] / acc[:, HEAD_DIM:]
    if kind == "b":
        lam = (jnp.exp(jnp.sum(lq1_ref[...] * lk1_ref[...], axis=-1, keepdims=True))
               - jnp.exp(jnp.sum(lq2_ref[...] * lk2_ref[...], axis=-1, keepdims=True)) + lam_init)
        o = o[:tq] - lam * o[tq:]
        o = o * lax.rsqrt(jnp.mean(o * o, axis=-1, keepdims=True) + NORM_EPS) * gain_ref[...]
        o_ref[...] = (o * (1.0 - lam_init)).astype(o_ref.dtype)
    else:
        for g in range(3):
            o_ref[:, g * HEAD_DIM:(g + 1) * HEAD_DIM] = o[g * tq:(g + 1) * tq].astype(o_ref.dtype)


def flash_attention(kind, p_q, p_lat, p_ctx, d, *, lams=None, gain=None, lam_init=0.0, tq=512, kv_chunk=1024):
    sizes, offs, _ = _proj_layout(d)
    b, t, _ = p_q.shape
    tq = _pick(t, tq, 128)
    qn, kn, vn = "q" + kind, "k" + kind, "v" + kind
    n_kv = sizes[kn]
    args, specs = [], []
    if kind == "b":
        half = HEAD_DIM // 2
        args += [v.reshape(1, half) for v in lams] + [gain.reshape(1, HEAD_DIM)]
        specs += [pl.BlockSpec((1, half), lambda bi, h, i: (0, 0))] * 4
        specs += [pl.BlockSpec((1, HEAD_DIM), lambda bi, h, i: (0, 0))]
        args.append(p_q)
        specs.append(pl.BlockSpec((None, tq, HEAD_DIM), lambda bi, h, i: (bi, i, offs[qn] + h)))
        out_w = HEAD_DIM
    else:
        for g in range(3):
            args.append(p_q)
            specs.append(pl.BlockSpec((None, tq, HEAD_DIM),
                                      lambda bi, h, i, g=g: (bi, i, offs[qn] + 3 * h + g)))
        out_w = 3 * HEAD_DIM

    def kv(arr, name):
        return pl.BlockSpec((None, arr.shape[1], HEAD_DIM), lambda bi, h, i: (bi, 0, offs[name] + h))

    if p_lat is not None:
        args += [p_lat, p_lat]
        specs += [kv(p_lat, kn), kv(p_lat, vn)]
    args += [p_ctx, p_ctx]
    specs += [kv(p_ctx, kn), kv(p_ctx, vn)]
    n_keys = p_ctx.shape[1] + (p_lat.shape[1] if p_lat is not None else 0)
    body = functools.partial(_flash_body, kind=kind, has_lat=p_lat is not None, tq=tq,
                             kv_chunk=kv_chunk, lam_init=lam_init)
    return pl.pallas_call(
        body, grid=(b, n_kv, t // tq), in_specs=specs,
        out_specs=pl.BlockSpec((None, tq, out_w), lambda bi, h, i: (bi, i, h)),
        out_shape=jax.ShapeDtypeStruct((b, t, sizes[qn] * HEAD_DIM), BF16),
        scratch_shapes=[pltpu.VMEM((n_keys, 2 * HEAD_DIM), BF16)],
        compiler_params=_cparams("arbitrary", "arbitrary", "arbitrary"), name="flash_" + kind,
    )(*args)


def _outproj_body(*refs, n_parts):
    a_refs, w_refs = refs[:n_parts], refs[n_parts:2 * n_parts]
    x_ref, gate_ref, o_ref = refs[2 * n_parts:]
    acc = jnp.dot(a_refs[0][...], w_refs[0][...], preferred_element_type=F32)
    for a_ref, w_ref in zip(a_refs[1:], w_refs[1:]):
        acc += jnp.dot(a_ref[...], w_ref[...], preferred_element_type=F32)
    o_ref[...] = x_ref[...] + gate_ref[...] * acc


def out_projection(parts, w_bf16, x, gate):
    b, t, d = x.shape
    m = b * t
    per_batch = gate.shape[0] == b
    tm = _pick(t, 1024, 512, 256, 128) if per_batch else _pick(m, 1024, 512, 256, 128)
    tn = _pick(d, 512, 256, 128)
    nt = t // tm if per_batch else 1
    gate_map = (lambda i, j: (i // nt, 0, j)) if per_batch else (lambda i, j: (0, 0, j))
    widths = [p.shape[-1] for p in parts]
    w_parts, lo = [], 0
    for k in widths:
        w_parts.append(w_bf16[lo:lo + k])
        lo += k
    assert lo == w_bf16.shape[0]
    out = pl.pallas_call(
        functools.partial(_outproj_body, n_parts=len(parts)), grid=(m // tm, d // tn),
        in_specs=([pl.BlockSpec((tm, k), lambda i, j: (i, 0)) for k in widths]
                  + [pl.BlockSpec((k, tn), lambda i, j: (0, j)) for k in widths]
                  + [pl.BlockSpec((tm, tn), lambda i, j: (i, j)),
                     pl.BlockSpec((None, 1, tn), gate_map)]),
        out_specs=pl.BlockSpec((tm, tn), lambda i, j: (i, j)),
        out_shape=jax.ShapeDtypeStruct((m, d), F32),
        compiler_params=_cparams("arbitrary", "arbitrary"), name="out_projection",
    )(*[p.reshape(m, k) for p, k in zip(parts, widths)], *w_parts, x.reshape(m, d), gate)
    return out.reshape(b, t, d)


def _moe_plan(gid, tm):
    n = gid.shape[0]
    n_tiles = n // tm + N_GROUPS - 1
    onehot = (gid[:, None] == jnp.arange(N_GROUPS, dtype=jnp.int32)[None, :]).astype(jnp.int32)
    cnt = jnp.sum(onehot, axis=0)
    tiles_g = (cnt + tm - 1) // tm
    tile_end = jnp.cumsum(tiles_g)
    tile_start = tile_end - tiles_g
    rank = jnp.take_along_axis(jnp.cumsum(onehot, axis=0), gid[:, None], axis=1)[:, 0] - 1
    slot = tile_start[gid] * tm + rank
    src = jnp.zeros((n_tiles * tm,), jnp.int32).at[slot].set(jnp.arange(n, dtype=jnp.int32))
    tiles = jnp.arange(n_tiles, dtype=jnp.int32)
    tile_grp = jnp.minimum(jnp.sum((tile_end[None, :] <= tiles[:, None]).astype(jnp.int32), axis=1),
                           N_GROUPS - 1)
    n_valid = jnp.clip(cnt[tile_grp] - (tiles - tile_start[tile_grp]) * tm, 0, tm).astype(jnp.int32)
    return src, tile_grp, n_valid


def _moe_body(src_ref, grp_ref, nv_ref, t_hbm, wg0_ref, wg1_ref, wu0_ref, wu1_ref, wd_ref, y_hbm,
              tbuf, acc_ref, gsem, ssem, *, tm, d, ff, fc):
    tile = pl.program_id(0)
    chunk = pl.program_id(1)
    n_tiles = pl.num_programs(0)
    n_chunks = pl.num_programs(1)
    slot = tile % 2

    def row_gather(tl, sl, r):
        return pltpu.make_async_copy(t_hbm.at[pl.ds(src_ref[tl * tm + r], 1)],
                                     tbuf.at[sl, pl.ds(r, 1)], gsem.at[sl])

    def row_scatter(tl, r):
        return pltpu.make_async_copy(acc_ref.at[pl.ds(r, 1)],
                                     y_hbm.at[pl.ds(src_ref[tl * tm + r], 1)], ssem)

    def for_rows(n, fn, **kw):
        def step(r, carry):
            fn(r)
            return carry
        lax.fori_loop(0, n, step, 0, **kw)

    def active(tl):
        return nv_ref[jnp.clip(tl, 0, n_tiles - 1)] > 0

    next_active = (tile + 1 < n_tiles) & active(tile + 1)

    @pl.when(active(tile))
    def _():
        @pl.when(chunk == 0)
        def _():
            @pl.when(tile == 0)
            def _():
                for_rows(tm, lambda r: row_gather(0, 0, r).start(), unroll=8)

            @pl.when(next_active)
            def _():
                for_rows(tm, lambda r: row_gather(tile + 1, 1 - slot, r).start(), unroll=8)

            @pl.when(tile > 0)
            def _():
                for_rows(nv_ref[jnp.maximum(tile - 1, 0)], lambda r: row_scatter(tile - 1, r).wait())

            acc_ref[...] = jnp.zeros_like(acc_ref)
            for_rows(tm, lambda r: row_gather(tile, slot, r).wait(), unroll=8)

        t = tbuf[slot, :, :d].astype(BF16)
        g = jnp.dot(t, jnp.concatenate([wg0_ref[...], wg1_ref[...]], axis=1), preferred_element_type=F32)
        u = jnp.dot(t, jnp.concatenate([wu0_ref[...], wu1_ref[...]], axis=1), preferred_element_type=F32)
        comb = tbuf[slot, :, d:]
        lane = lax.broadcasted_iota(jnp.int32, comb.shape, 1)
        lane0 = EXPERT_LANE0 + EXPERTS_PER_GROUP * grp_ref[tile]
        cw = [jnp.sum(jnp.where(lane == lane0 + e, comb, 0.0), axis=-1, keepdims=True)
              for e in range(EXPERTS_PER_GROUP)]
        col = chunk * fc + lax.broadcasted_iota(jnp.int32, (1, fc), 1)
        cw_col = cw[EXPERTS_PER_GROUP - 1]
        for e in range(EXPERTS_PER_GROUP - 2, -1, -1):
            cw_col = jnp.where(col < (e + 1) * ff, cw[e], cw_col)
        h = (g * jax.nn.sigmoid(g)) * u * cw_col
        acc_ref[...] += jnp.dot(h.astype(BF16), wd_ref[...], preferred_element_type=F32)

        @pl.when(chunk == n_chunks - 1)
        def _():
            n_valid = nv_ref[tile]

            @pl.when(n_valid == tm)
            def _():
                for_rows(tm, lambda r: row_scatter(tile, r).start(), unroll=8)

            @pl.when(n_valid < tm)
            def _():
                for_rows(n_valid, lambda r: row_scatter(tile, r).start())

            @pl.when(jnp.logical_not(next_active))
            def _():
                for_rows(n_valid, lambda r: row_scatter(tile, r).wait())


def moe_experts(t_ext, gid, wg, wu, wd):
    m, dw = t_ext.shape
    d = dw - ROUTER_LANES
    n_e, _, ff = wg.shape
    gff = EXPERTS_PER_GROUP * ff
    tiles_per_expert = ff // HEAD_DIM
    assert ff % HEAD_DIM == 0 and (EXPERTS_PER_GROUP * tiles_per_expert) % 2 == 0
    fc = 2 * HEAD_DIM
    tm = 512 if m % 512 == 0 and m >= 8192 else _pick(m, 256, 128)
    src, tile_grp, n_valid = _moe_plan(gid, tm)
    n_tiles = tile_grp.shape[0]

    def col_tile(k):
        def index_map(i, c, src, grp, nv):
            g = 2 * c + k
            return (grp[i] * EXPERTS_PER_GROUP + g // tiles_per_expert, 0, g % tiles_per_expert)
        return pl.BlockSpec((None, d, HEAD_DIM), index_map)

    wd_grp = wd.reshape(N_GROUPS, gff, d)
    grid_spec = pltpu.PrefetchScalarGridSpec(
        num_scalar_prefetch=3, grid=(n_tiles, gff // fc),
        in_specs=[pl.BlockSpec(memory_space=pl.ANY),
                  col_tile(0), col_tile(1), col_tile(0), col_tile(1),
                  pl.BlockSpec((None, fc, d), lambda i, c, src, grp, nv: (grp[i], c, 0))],
        out_specs=pl.BlockSpec(memory_space=pl.ANY),
        scratch_shapes=[pltpu.VMEM((2, tm, dw), F32), pltpu.VMEM((tm, d), F32),
                        pltpu.SemaphoreType.DMA((2,)), pltpu.SemaphoreType.DMA(())])
    return pl.pallas_call(
        functools.partial(_moe_body, tm=tm, d=d, ff=ff, fc=fc), grid_spec=grid_spec,
        out_shape=jax.ShapeDtypeStruct((m, d), F32),
        compiler_params=_cparams("arbitrary", "arbitrary"), name="moe_experts",
    )(src, tile_grp, n_valid, t_ext, wg, wg, wu, wu, wd_grp)


def _router_weights(w_rg, b_rg, w_re, b_re):
    d = w_rg.shape[0]
    n_used = N_GROUPS + N_EXPERTS
    w = jnp.concatenate([w_rg, jnp.transpose(w_re, (1, 0, 2)).reshape(d, N_EXPERTS),
                         jnp.zeros((d, ROUTER_LANES - n_used), F32)], axis=1)
    bias = jnp.concatenate([b_rg, b_re.reshape(N_EXPERTS), jnp.zeros((ROUTER_LANES - n_used,), F32)])
    w_hi = w.astype(BF16)
    w_lo = (w - w_hi.astype(F32)).astype(BF16)
    return jnp.concatenate([w_hi, w_lo], axis=1), bias.reshape(1, ROUTER_LANES)


def kernel(x, c, ctx, c_ctx, ada_w, ada_b, norm_mix, w_in, sink_a, lambda_q1, lambda_k1, lambda_q2,
           lambda_k2, subln_b, q_norm_c, k_norm_c, w_out, norm_ffn, w_router_grp, b_router_grp,
           w_router_exp, b_router_exp, w_gate, w_up, w_down, final_norm):
    bx, n_lat, d = x.shape
    n_ctx = ctx.shape[1]
    depth = ada_w.shape[0]
    sizes, offs, _ = _proj_layout(d)

    n_cond = bx + 1
    rows = -(-n_cond // 8) * 8
    cond = jnp.concatenate([c, c_ctx[None], jnp.zeros((rows - n_cond, d), F32)], axis=0)
    mods = adaln_all(cond, ada_w, ada_b)

    tabs = (_rope_tables(n_lat, False), _rope_tables(n_lat, True))
    perms = _head_permutations(d)

    xc = ctx
    pend = None
    pend_c = None
    for l in range(depth):
        last = l == depth - 1
        lam_init = 0.8 - 0.6 * math.exp(-0.3 * l)
        lat = [mods[l, :bx, None, i * d:(i + 1) * d] for i in range(N_MOD)]
        cm = [mods[l, bx:bx + 1, None, i * d:(i + 1) * d] for i in range(N_MOD)]
        w_in_l = cast_permute_proj(w_in[l], perms)
        gq = _permute_head_lanes(q_norm_c[l], _PERM_AC)
        gk = _permute_head_lanes(k_norm_c[l], _PERM_AC)
        w_out_l = w_out[l].astype(BF16)
        lams = (lambda_q1[l], lambda_k1[l], lambda_q2[l], lambda_k2[l])

        outs = norm_stage(x, norm_mix[l], res=pend, mod=(lat[0], lat[1]))
        if pend is not None:
            x, u = outs
        else:
            (u,) = outs
        outs = norm_stage(xc, norm_mix[l], res=pend_c, mod=(cm[0], cm[1]))
        if pend_c is not None:
            xc, uc = outs
        else:
            (uc,) = outs

        p = in_projection(u, w_in_l, gq, gk, rope_tabs=tabs)
        pc = in_projection(uc, w_in_l, gq, gk)

        oa = attention_a(p, p, pc, sink_a[l], d, local=True)
        ob = flash_attention("b", p, p, pc, d, lams=lams, gain=subln_b[l], lam_init=lam_init, tq=1024)
        oc = flash_attention("c", p, p, pc, d)
        x = out_projection((oa, ob, oc), w_out_l, x, lat[2])

        if not last:
            oa_c = attention_a(pc, pc, pc, sink_a[l], d, local=False)
            ob_c = flash_attention("b", pc, None, pc, d, lams=lams, gain=subln_b[l], lam_init=lam_init)
            oc_c = flash_attention("c", pc, None, pc, d)
            xc = out_projection((oa_c, ob_c, oc_c), w_out_l, xc, cm[2])

        router = _router_weights(w_router_grp[l], b_router_grp[l], w_router_exp[l], b_router_exp[l])
        wg, wu, wd = w_gate[l].astype(BF16), w_up[l].astype(BF16), w_down[l].astype(BF16)
        t, gid = norm_stage(x, norm_ffn[l], mod=(lat[3], lat[4]), router=router)
        y = moe_experts(t.reshape(bx * n_lat, d + ROUTER_LANES), gid[:, :, 0].reshape(bx * n_lat), wg, wu, wd)
        pend = (y.reshape(bx, n_lat, d), lat[5])
        if not last:
            tc, gid_c = norm_stage(xc, norm_ffn[l], mod=(cm[3], cm[4]), router=router)
            yc = moe_experts(tc.reshape(bx * n_ctx, d + ROUTER_LANES), gid_c[:, :, 0].reshape(bx * n_ctx),
                             wg, wu, wd)
            pend_c = (yc.reshape(bx, n_ctx, d), cm[5])

    (out,) = norm_stage(x, final_norm, res=pend, keep_x=False, out_dtype=F32)
    return out
```

```python
import functools
import math

import jax
import jax.numpy as jnp
from jax import lax
from jax.experimental import pallas as pl
from jax.experimental.pallas import tpu as pltpu

F32 = jnp.float32
BF16 = jnp.bfloat16

HEAD_DIM = 128
GRID_W = 64
WINDOW = 128
ROPE_THETA = 10000.0
ATT_SCALE = HEAD_DIM ** -0.5
DIFF_SCALE = (HEAD_DIM // 2) ** -0.5
LOG2E = math.log2(math.e)
N_GROUPS = 4
EXPERTS_PER_GROUP = 4
N_EXPERTS = N_GROUPS * EXPERTS_PER_GROUP
N_MOD = 6
NORM_EPS = 1e-6
NEG_INF = -1e30
ROUTER_LANES = 128
EXPERT_LANE0 = N_GROUPS

V7X_VMEM_LIMIT_BYTES = 52 * 1024 * 1024


def _cparams(*sem):
    return pltpu.CompilerParams(dimension_semantics=sem, vmem_limit_bytes=V7X_VMEM_LIMIT_BYTES)


def _pick(n, *cands):
    for c in cands:
        if n % c == 0:
            return c
    return n


def _adaln_body(cond_ref, w_ref, b_ref, o_ref):
    c = cond_ref[...]
    s = (c * jax.nn.sigmoid(c)).astype(BF16)
    o_ref[...] = jnp.dot(s, w_ref[...].astype(BF16), preferred_element_type=F32) + b_ref[...]


def adaln_all(cond, ada_w, ada_b):
    n_layers, d, n6 = ada_w.shape
    r = cond.shape[0]
    tn = _pick(n6, 512, 256, 128)
    return pl.pallas_call(
        _adaln_body,
        grid=(n_layers, n6 // tn),
        in_specs=[
            pl.BlockSpec((r, d), lambda l, j: (0, 0)),
            pl.BlockSpec((None, d, tn), lambda l, j: (l, 0, j)),
            pl.BlockSpec((None, 1, tn), lambda l, j: (l, 0, j)),
        ],
        out_specs=pl.BlockSpec((None, r, tn), lambda l, j: (l, 0, j)),
        out_shape=jax.ShapeDtypeStruct((n_layers, r, n6), F32),
        compiler_params=_cparams("arbitrary", "arbitrary"),
        name="adaln",
    )(cond, ada_w, ada_b.reshape(n_layers, 1, n6))


def _norm_body(*refs, has_res, keep_x, has_mod, has_router):
    it = iter(refs)
    x_ref = next(it)
    if has_res:
        y_ref, gate_ref = next(it), next(it)
    g_ref = next(it)
    if has_mod:
        sh_ref, sc_ref = next(it), next(it)
    if has_router:
        whl_ref, rb_ref = next(it), next(it)
    if keep_x:
        xo_ref = next(it)
    u_ref = next(it)
    if has_router:
        gid_ref = next(it)

    x = x_ref[...]
    if has_res:
        x = x + gate_ref[...] * y_ref[...].astype(F32)
    if keep_x:
        xo_ref[...] = x
    xn = x * lax.rsqrt(jnp.mean(x * x, axis=-1, keepdims=True) + NORM_EPS) * g_ref[...]
    if has_mod:
        xn = xn * (1.0 + sc_ref[...]) + sh_ref[...]
    if not has_router:
        u_ref[...] = xn.astype(u_ref.dtype)
    else:
        d = xn.shape[-1]
        u_ref[:, :d] = xn
        th = xn.astype(BF16)
        tl = (xn - th.astype(F32)).astype(BF16)
        hi = jnp.dot(th, whl_ref[...], preferred_element_type=F32)
        logits = (hi[:, :ROUTER_LANES] + hi[:, ROUTER_LANES:]
                  + jnp.dot(tl, whl_ref[:, :ROUTER_LANES], preferred_element_type=F32)) + rb_ref[...]
        lane = lax.broadcasted_iota(jnp.int32, logits.shape, 1)
        big = jnp.int32(ROUTER_LANES)
        lg = jnp.where(lane < N_GROUPS, logits, -jnp.inf)
        gmax = jnp.max(lg, axis=-1, keepdims=True)
        p_grp = 1.0 / jnp.sum(jnp.exp(lg - gmax), axis=-1, keepdims=True)
        gidx = jnp.min(jnp.where(lg == gmax, lane, big), axis=-1, keepdims=True)
        e_lo = EXPERT_LANE0 + EXPERTS_PER_GROUP * gidx
        le = jnp.where((lane >= e_lo) & (lane < e_lo + EXPERTS_PER_GROUP), logits, -jnp.inf)
        l1 = jnp.max(le, axis=-1, keepdims=True)
        i1 = jnp.min(jnp.where(le == l1, lane, big), axis=-1, keepdims=True)
        le2 = jnp.where(lane == i1, -jnp.inf, le)
        l2 = jnp.max(le2, axis=-1, keepdims=True)
        i2 = jnp.min(jnp.where(le2 == l2, lane, big), axis=-1, keepdims=True)
        e2 = jnp.exp(l2 - l1)
        w1 = p_grp / (1.0 + e2)
        w2 = w1 * e2
        comb = jnp.where(lane == i1, w1, jnp.where(lane == i2, w2, 0.0))
        u_ref[:, d:] = comb
        gid_ref[...] = jnp.broadcast_to(gidx, gid_ref.shape)


def norm_stage(x, gain, *, res=None, keep_x=None, mod=None, router=None, out_dtype=BF16):
    keep_x = (res is not None) if keep_x is None else keep_x
    b, t, d = x.shape
    ts = _pick(t, 256, 128, 64, 32, 16, 8)
    row = pl.BlockSpec((None, ts, d), lambda bi, i: (bi, i, 0))

    def per_batch(arr):
        if arr.shape[0] == b:
            return pl.BlockSpec((None, 1, d), lambda bi, i: (bi, 0, 0))
        return pl.BlockSpec((None, 1, d), lambda bi, i: (0, 0, 0))

    args, specs = [x], [row]
    if res is not None:
        y, gate = res
        args += [y, gate]
        specs += [row, per_batch(gate)]
    args.append(gain.reshape(1, d))
    specs.append(pl.BlockSpec((1, d), lambda bi, i: (0, 0)))
    if mod is not None:
        args += [mod[0], mod[1]]
        specs += [per_batch(mod[0]), per_batch(mod[1])]
    if router is not None:
        args += list(router)
        specs += [pl.BlockSpec((d, 2 * ROUTER_LANES), lambda bi, i: (0, 0)),
                  pl.BlockSpec((1, ROUTER_LANES), lambda bi, i: (0, 0))]
    out_shapes, out_specs = [], []
    if keep_x:
        out_shapes.append(jax.ShapeDtypeStruct((b, t, d), F32))
        out_specs.append(row)
    if router is None:
        out_shapes.append(jax.ShapeDtypeStruct((b, t, d), out_dtype))
        out_specs.append(row)
    else:
        out_shapes.append(jax.ShapeDtypeStruct((b, t, d + ROUTER_LANES), F32))
        out_specs.append(pl.BlockSpec((None, ts, d + ROUTER_LANES), lambda bi, i: (bi, i, 0)))
        out_shapes.append(jax.ShapeDtypeStruct((b, t, ROUTER_LANES), jnp.int32))
        out_specs.append(pl.BlockSpec((None, ts, ROUTER_LANES), lambda bi, i: (bi, i, 0)))
    body = functools.partial(_norm_body, has_res=res is not None, keep_x=keep_x, has_mod=mod is not None,
                             has_router=router is not None)
    return pl.pallas_call(
        body, grid=(b, t // ts), in_specs=specs, out_specs=out_specs, out_shape=out_shapes,
        compiler_params=_cparams("arbitrary", "arbitrary"), name="norm_stage",
    )(*args)


_PERM_AC = ((2, 2, HEAD_DIM // 4), (1, 0, 2))
_PERM_B = ((2, 2, 2, HEAD_DIM // 8), (2, 0, 1, 3))


def _permute_head_lanes(arr, perm):
    split, order = perm
    lead = arr.shape[:-1]
    nl = len(lead)
    out = arr.reshape(lead + split).transpose(tuple(range(nl)) + tuple(nl + o for o in order))
    return out.reshape(lead + (HEAD_DIM,))


def _head_permutations(d):
    sizes, _, _ = _proj_layout(d)
    eye = jnp.eye(HEAD_DIM, dtype=BF16)
    mats = []
    for name, n in sizes.items():
        p = eye if name[0] == "v" else _permute_head_lanes(eye, _PERM_B if name[1] == "b" else _PERM_AC)
        mats.append(jnp.broadcast_to(p, (n, HEAD_DIM, HEAD_DIM)))
    return jnp.concatenate(mats, axis=0)


def _cast_permute_body(w_ref, p_ref, o_ref):
    w = w_ref[...].astype(BF16)
    for h in range(p_ref.shape[0]):
        sl = slice(h * HEAD_DIM, (h + 1) * HEAD_DIM)
        o_ref[:, sl] = jnp.dot(w[:, sl], p_ref[h], preferred_element_type=F32).astype(BF16)


def cast_permute_proj(w_all, layer, perms):
    _, d, pw = w_all.shape
    tr = _pick(d, 1024, 512, 256, 128)
    hpt = _pick(pw // HEAD_DIM, 4, 2, 1)
    tn = hpt * HEAD_DIM
    return pl.pallas_call(
        _cast_permute_body, grid=(d // tr, pw // tn),
        in_specs=[pl.BlockSpec((None, tr, tn), lambda i, j: (layer, i, j)),
                  pl.BlockSpec((hpt, HEAD_DIM, HEAD_DIM), lambda i, j: (j, 0, 0))],
        out_specs=pl.BlockSpec((tr, tn), lambda i, j: (i, j)),
        out_shape=jax.ShapeDtypeStruct((d, pw), BF16),
        compiler_params=_cparams("arbitrary", "arbitrary"), name="cast_permute_proj",
    )(w_all, perms)


def _rope_tables(seq, group_b):
    lane = jnp.arange(HEAD_DIM)
    if group_b:
        axis, f, block = (lane % 32) // 16, lane % 16, HEAD_DIM // 4
    else:
        axis, f, block = (lane % 64) // 32, lane % 32, HEAD_DIM // 2
    inv = ROPE_THETA ** (-(2.0 * f.astype(F32)) / block)
    t = jnp.arange(seq)
    pos = jnp.where((axis == 0)[None, :], (t // GRID_W)[:, None], (t % GRID_W)[:, None])
    ang = pos.astype(F32) * inv[None, :]
    return jnp.stack([jnp.cos(ang), jnp.where((lane < HEAD_DIM // 2)[None, :], -1.0, 1.0) * jnp.sin(ang)])


def _inproj_body(*refs, segs, rope, heads_per_tile):
    if rope:
        a_ref, w_ref, gq_ref, gk_ref, tab_ac_ref, tab_b_ref, o_ref, acc_ref = refs
    else:
        a_ref, w_ref, gq_ref, gk_ref, o_ref, acc_ref = refs
    j = pl.program_id(1)

    norm_lo = min(lo for lo, _, kind in segs if kind in ("qc", "kc"))
    norm_hi = max(hi for _, hi, kind in segs if kind in ("qc", "kc"))

    @pl.when((j >= norm_lo) & (j < norm_hi))
    def _():
        acc_ref[...] = jnp.dot(a_ref[...], w_ref[...], preferred_element_type=F32)

    def emit(kind):
        normed = kind in ("qc", "kc")
        sub = heads_per_tile if normed else min(heads_per_tile, 2)
        for c in range(heads_per_tile // sub):
            c0 = c * sub * HEAD_DIM
            if not normed:
                acc = jnp.dot(a_ref[...], w_ref[:, c0:c0 + sub * HEAD_DIM], preferred_element_type=F32)
            for h in range(sub):
                y = acc_ref[:, h * HEAD_DIM:(h + 1) * HEAD_DIM] if normed else acc[:, h * HEAD_DIM:(h + 1) * HEAD_DIM]
                if kind in ("qc", "kc"):
                    g = gq_ref[...] if kind == "qc" else gk_ref[...]
                    y = y * lax.rsqrt(jnp.mean(y * y, axis=-1, keepdims=True) + NORM_EPS) * g
                if rope and kind != "v":
                    tab = tab_b_ref if kind in ("qb", "kb") else tab_ac_ref
                    y = y * tab[0] + pltpu.roll(y, HEAD_DIM // 2, 1) * tab[1]
                if kind in ("qa", "qc"):
                    y = y * (ATT_SCALE * LOG2E)
                elif kind == "qb":
                    y = y * (DIFF_SCALE * LOG2E)
                o_ref[:, c0 + h * HEAD_DIM:c0 + (h + 1) * HEAD_DIM] = y.astype(o_ref.dtype)

    for lo, hi, kind in segs:
        pl.when((j >= lo) & (j < hi))(functools.partial(emit, kind))


def _proj_layout(d):
    h = d // HEAD_DIM
    a_h = 3 * h // 8
    a_kv = a_h // 3
    b_h = h // 4
    c_h = h - a_h - b_h
    c_kv = c_h // 3
    sizes = [a_h, a_kv, a_kv, b_h, b_h, b_h, c_h, c_kv, c_kv]
    names = ["qa", "ka", "va", "qb", "kb", "vb", "qc", "kc", "vc"]
    offs, acc = {}, 0
    for n, s in zip(names, sizes):
        offs[n] = acc
        acc += s
    return dict(zip(names, sizes)), offs, acc


def in_projection(u, w_bf16, q_gain, k_gain, *, rope_tabs=None):
    b, t, d = u.shape
    pw = w_bf16.shape[1]
    sizes, offs, total = _proj_layout(d)
    assert total * HEAD_DIM == pw
    m = b * t
    tm = _pick(t, 1024, 512, 256, 128) if rope_tabs is not None else _pick(m, 1024, 512, 256, 128)
    g = functools.reduce(math.gcd, sizes.values())
    hpt = _pick(g, 4, 2, 1)
    tn = hpt * HEAD_DIM
    kinds = {"qa": "qa", "ka": "ka", "va": "v", "qb": "qb", "kb": "kb", "vb": "v",
             "qc": "qc", "kc": "kc", "vc": "v"}
    segs = tuple((offs[n] // hpt, (offs[n] + sizes[n]) // hpt, kinds[n]) for n in sizes)
    in_specs = [
        pl.BlockSpec((tm, d), lambda i, j: (i, 0)),
        pl.BlockSpec((d, tn), lambda i, j: (0, j)),
        pl.BlockSpec((1, HEAD_DIM), lambda i, j: (0, 0)),
        pl.BlockSpec((1, HEAD_DIM), lambda i, j: (0, 0)),
    ]
    args = [u.reshape(m, d), w_bf16, q_gain.reshape(1, HEAD_DIM), k_gain.reshape(1, HEAD_DIM)]
    if rope_tabs is not None:
        nt = t // tm
        tab_spec = pl.BlockSpec((2, tm, HEAD_DIM), lambda i, j: (0, i % nt, 0))
        in_specs += [tab_spec, tab_spec]
        args += list(rope_tabs)
    body = functools.partial(_inproj_body, segs=segs, rope=rope_tabs is not None, heads_per_tile=hpt)
    out = pl.pallas_call(
        body, grid=(m // tm, pw // tn), in_specs=in_specs,
        out_specs=pl.BlockSpec((tm, tn), lambda i, j: (i, j)),
        out_shape=jax.ShapeDtypeStruct((m, pw), BF16),
        scratch_shapes=[pltpu.VMEM((tm, tn), F32)],
        compiler_params=_cparams("arbitrary", "arbitrary"), name="in_projection",
    )(*args)
    return out.reshape(b, t, pw)


def _attn_a_body(sink_ref, q0_ref, q1_ref, q2_ref, kl_ref, vl_ref, kc_ref, vc_ref, o_ref, *, local, tq, wlen):
    hkv = pl.program_id(1)
    qi = pl.program_id(2)
    q = jnp.concatenate([q0_ref[...], q1_ref[...], q2_ref[...]], axis=0)
    nt = (((1,), (1,)), ((), ()))
    s_ctx = lax.dot_general(q, kc_ref[...], nt, preferred_element_type=F32)
    row = lax.broadcasted_iota(jnp.int32, (3 * tq, 1), 0)
    sink = LOG2E * jnp.where(row < tq, sink_ref[3 * hkv],
                             jnp.where(row < 2 * tq, sink_ref[3 * hkv + 1], sink_ref[3 * hkv + 2]))
    m = jnp.maximum(jnp.max(s_ctx, axis=-1, keepdims=True), sink)
    if local:
        n_lat = kl_ref.shape[0]
        start = pl.multiple_of(jnp.clip(qi * tq - WINDOW, 0, n_lat - wlen), WINDOW)
        k_win = kl_ref[pl.ds(start, wlen), :]
        v_win = vl_ref[pl.ds(start, wlen), :]
        s_loc = lax.dot_general(q, k_win, nt, preferred_element_type=F32)
        kpos = start + lax.broadcasted_iota(jnp.int32, (1, wlen), 1)
        qpos = qi * tq + jnp.where(row < tq, row, jnp.where(row < 2 * tq, row - tq, row - 2 * tq))
        s_loc = jnp.where(jnp.abs(kpos - qpos) <= WINDOW, s_loc, NEG_INF)
        m = jnp.maximum(m, jnp.max(s_loc, axis=-1, keepdims=True))
    p_ctx = jnp.exp2(s_ctx - m)
    l = jnp.sum(p_ctx, axis=-1, keepdims=True) + jnp.exp2(sink - m)
    o = jnp.dot(p_ctx.astype(BF16), vc_ref[...], preferred_element_type=F32)
    if local:
        p_loc = jnp.exp2(s_loc - m)
        l = l + jnp.sum(p_loc, axis=-1, keepdims=True)
        o = o + jnp.dot(p_loc.astype(BF16), v_win, preferred_element_type=F32)
    o = o / l
    for g in range(3):
        o_ref[:, g * HEAD_DIM:(g + 1) * HEAD_DIM] = o[g * tq:(g + 1) * tq].astype(o_ref.dtype)


def attention_a(p_q, p_lat, p_ctx, sink, d, *, local):
    sizes, offs, _ = _proj_layout(d)
    b, t, _ = p_q.shape
    n_ctx = p_ctx.shape[1]
    n_lat = p_lat.shape[1]
    tq = _pick(t, 256, 128)
    wlen = tq + 2 * WINDOW
    if local:
        assert n_lat >= wlen
    grid = (b, sizes["ka"], t // tq)

    def qspec(g):
        return pl.BlockSpec((None, tq, HEAD_DIM), lambda bi, h, i: (bi, i, offs["qa"] + 3 * h + g))

    def kvspec(n, name):
        return pl.BlockSpec((None, n, HEAD_DIM), lambda bi, h, i: (bi, 0, offs[name] + h))

    body = functools.partial(_attn_a_body, local=local, tq=tq, wlen=wlen)
    return pl.pallas_call(
        body, grid=grid,
        in_specs=[pl.BlockSpec(memory_space=pltpu.SMEM), qspec(0), qspec(1), qspec(2),
                  kvspec(n_lat, "ka"), kvspec(n_lat, "va"), kvspec(n_ctx, "ka"), kvspec(n_ctx, "va")],
        out_specs=pl.BlockSpec((None, tq, 3 * HEAD_DIM), lambda bi, h, i: (bi, i, h)),
        out_shape=jax.ShapeDtypeStruct((b, t, sizes["qa"] * HEAD_DIM), BF16),
        compiler_params=_cparams("arbitrary", "arbitrary", "arbitrary"), name="attention_a",
    )(sink, p_q, p_q, p_q, p_lat, p_lat, p_ctx, p_ctx)


def _flash_body(*refs, kind, has_lat, tq, kv_chunk, lam_init, n_chains):
    it = iter(refs)
    if kind == "b":
        lq1_ref, lk1_ref, lq2_ref, lk2_ref, gain_ref = (next(it) for _ in range(5))
        q = next(it)[...]
        lane = lax.broadcasted_iota(jnp.int32, q.shape, 1)
        sub1 = (lane % (HEAD_DIM // 2)) < HEAD_DIM // 4
        zero = jnp.zeros_like(q)
        q = jnp.concatenate([jnp.where(sub1, q, zero), jnp.where(sub1, zero, q)], axis=0)
    else:
        q = jnp.concatenate([next(it)[...], next(it)[...], next(it)[...]], axis=0)
    if has_lat:
        kl_ref, vl_ref = next(it), next(it)
    kc_ref, vc_ref, o_ref, vaug_ref = next(it), next(it), next(it), next(it)

    n_lat = kl_ref.shape[0] if has_lat else 0
    n_ctx = kc_ref.shape[0]

    @pl.when(pl.program_id(2) == 0)
    def _():
        if has_lat:
            vaug_ref[0:n_lat, 0:HEAD_DIM] = vl_ref[...]
        vaug_ref[n_lat:n_lat + n_ctx, 0:HEAD_DIM] = vc_ref[...]
        vaug_ref[:, HEAD_DIM:] = jnp.ones((n_lat + n_ctx, HEAD_DIM), BF16)

    chunks = [(kl_ref, lo, lo, min(kv_chunk, n_lat - lo)) for lo in range(0, n_lat, kv_chunk)]
    chunks.append((kc_ref, 0, n_lat, n_ctx))

    rows = q.shape[0] // n_chains
    qs = [q[c * rows:(c + 1) * rows] for c in range(n_chains)]
    ms = [jnp.full((rows, 1), -jnp.inf, F32) for _ in range(n_chains)]
    accs = [jnp.zeros((rows, 2 * HEAD_DIM), F32) for _ in range(n_chains)]
    for k_ref, lo, vlo, n in chunks:
        for c in range(n_chains):
            s = lax.dot_general(qs[c], k_ref[lo:lo + n, :], (((1,), (1,)), ((), ())),
                                preferred_element_type=F32)
            m_new = jnp.maximum(ms[c], jnp.max(s, axis=-1, keepdims=True))
            p = jnp.exp2(s - m_new)
            accs[c] = jnp.exp2(ms[c] - m_new) * accs[c] + jnp.dot(
                p.astype(BF16), vaug_ref[vlo:vlo + n, :], preferred_element_type=F32)
            ms[c] = m_new
    acc = jnp.concatenate(accs, axis=0)
    o = acc[:, :HEAD_DIM] / acc[:, HEAD_DIM:]
    if kind == "b":
        lam = (jnp.exp(jnp.sum(lq1_ref[...] * lk1_ref[...], axis=-1, keepdims=True))
               - jnp.exp(jnp.sum(lq2_ref[...] * lk2_ref[...], axis=-1, keepdims=True)) + lam_init)
        o = o[:tq] - lam * o[tq:]
        o = o * lax.rsqrt(jnp.mean(o * o, axis=-1, keepdims=True) + NORM_EPS) * gain_ref[...]
        o_ref[...] = (o * (1.0 - lam_init)).astype(o_ref.dtype)
    else:
        for g in range(3):
            o_ref[:, g * HEAD_DIM:(g + 1) * HEAD_DIM] = o[g * tq:(g + 1) * tq].astype(o_ref.dtype)


def flash_attention(kind, p_q, p_lat, p_ctx, d, *, lams=None, gain=None, lam_init=0.0, tq=512, kv_chunk=1024,
                    n_chains=1):
    sizes, offs, _ = _proj_layout(d)
    b, t, _ = p_q.shape
    tq = _pick(t, tq, 128)
    qn, kn, vn = "q" + kind, "k" + kind, "v" + kind
    n_kv = sizes[kn]
    args, specs = [], []
    if kind == "b":
        half = HEAD_DIM // 2
        args += [v.reshape(1, half) for v in lams] + [gain.reshape(1, HEAD_DIM)]
        specs += [pl.BlockSpec((1, half), lambda bi, h, i: (0, 0))] * 4
        specs += [pl.BlockSpec((1, HEAD_DIM), lambda bi, h, i: (0, 0))]
        args.append(p_q)
        specs.append(pl.BlockSpec((None, tq, HEAD_DIM), lambda bi, h, i: (bi, i, offs[qn] + h)))
        out_w = HEAD_DIM
    else:
        for g in range(3):
            args.append(p_q)
            specs.append(pl.BlockSpec((None, tq, HEAD_DIM),
                                      lambda bi, h, i, g=g: (bi, i, offs[qn] + 3 * h + g)))
        out_w = 3 * HEAD_DIM

    def kv(arr, name):
        return pl.BlockSpec((None, arr.shape[1], HEAD_DIM), lambda bi, h, i: (bi, 0, offs[name] + h))

    if p_lat is not None:
        args += [p_lat, p_lat]
        specs += [kv(p_lat, kn), kv(p_lat, vn)]
    args += [p_ctx, p_ctx]
    specs += [kv(p_ctx, kn), kv(p_ctx, vn)]
    n_keys = p_ctx.shape[1] + (p_lat.shape[1] if p_lat is not None else 0)
    body = functools.partial(_flash_body, kind=kind, has_lat=p_lat is not None, tq=tq,
                             kv_chunk=kv_chunk, lam_init=lam_init, n_chains=n_chains)
    return pl.pallas_call(
        body, grid=(b, n_kv, t // tq), in_specs=specs,
        out_specs=pl.BlockSpec((None, tq, out_w), lambda bi, h, i: (bi, i, h)),
        out_shape=jax.ShapeDtypeStruct((b, t, sizes[qn] * HEAD_DIM), BF16),
        scratch_shapes=[pltpu.VMEM((n_keys, 2 * HEAD_DIM), BF16)],
        compiler_params=_cparams("arbitrary", "arbitrary", "arbitrary"), name="flash_" + kind,
    )(*args)


def _outproj_body(*refs, n_parts):
    a_refs, w_refs = refs[:n_parts], refs[n_parts:2 * n_parts]
    x_ref, gate_ref, o_ref = refs[2 * n_parts:]
    acc = jnp.dot(a_refs[0][...], w_refs[0][...], preferred_element_type=F32)
    for a_ref, w_ref in zip(a_refs[1:], w_refs[1:]):
        acc += jnp.dot(a_ref[...], w_ref[...], preferred_element_type=F32)
    o_ref[...] = x_ref[...] + gate_ref[...] * acc


def out_projection(parts, w_bf16, x, gate):
    b, t, d = x.shape
    m = b * t
    per_batch = gate.shape[0] == b
    tm = _pick(t, 1024, 512, 256, 128) if per_batch else _pick(m, 1024, 512, 256, 128)
    tn = _pick(d, 512, 256, 128)
    nt = t // tm if per_batch else 1
    gate_map = (lambda i, j: (i // nt, 0, j)) if per_batch else (lambda i, j: (0, 0, j))
    widths = [p.shape[-1] for p in parts]
    w_parts, lo = [], 0
    for k in widths:
        w_parts.append(w_bf16[lo:lo + k])
        lo += k
    assert lo == w_bf16.shape[0]
    out = pl.pallas_call(
        functools.partial(_outproj_body, n_parts=len(parts)), grid=(m // tm, d // tn),
        in_specs=([pl.BlockSpec((tm, k), lambda i, j: (i, 0)) for k in widths]
                  + [pl.BlockSpec((k, tn), lambda i, j: (0, j)) for k in widths]
                  + [pl.BlockSpec((tm, tn), lambda i, j: (i, j)),
                     pl.BlockSpec((None, 1, tn), gate_map)]),
        out_specs=pl.BlockSpec((tm, tn), lambda i, j: (i, j)),
        out_shape=jax.ShapeDtypeStruct((m, d), F32),
        compiler_params=_cparams("arbitrary", "arbitrary"), name="out_projection",
    )(*[p.reshape(m, k) for p, k in zip(parts, widths)], *w_parts, x.reshape(m, d), gate)
    return out.reshape(b, t, d)


def _moe_plan(gid, tm):
    n = gid.shape[0]
    n_tiles = n // tm + N_GROUPS - 1
    onehot = (gid[:, None] == jnp.arange(N_GROUPS, dtype=jnp.int32)[None, :]).astype(jnp.int32)
    cnt = jnp.sum(onehot, axis=0)
    tiles_g = (cnt + tm - 1) // tm
    tile_end = jnp.cumsum(tiles_g)
    tile_start = tile_end - tiles_g
    rank = jnp.take_along_axis(jnp.cumsum(onehot, axis=0), gid[:, None], axis=1)[:, 0] - 1
    slot = tile_start[gid] * tm + rank
    src = jnp.zeros((n_tiles * tm,), jnp.int32).at[slot].set(jnp.arange(n, dtype=jnp.int32))
    tiles = jnp.arange(n_tiles, dtype=jnp.int32)
    tile_grp = jnp.minimum(jnp.sum((tile_end[None, :] <= tiles[:, None]).astype(jnp.int32), axis=1),
                           N_GROUPS - 1)
    n_valid = jnp.clip(cnt[tile_grp] - (tiles - tile_start[tile_grp]) * tm, 0, tm).astype(jnp.int32)
    return src, tile_grp, n_valid


def _moe_body(src_ref, grp_ref, nv_ref, t_hbm, wg0_ref, wg1_ref, wu0_ref, wu1_ref, wd_ref, y_hbm,
              tbuf, acc_ref, gsem, ssem, *, tm, d, ff, fc):
    tile = pl.program_id(0)
    chunk = pl.program_id(1)
    n_tiles = pl.num_programs(0)
    n_chunks = pl.num_programs(1)
    slot = tile % 2

    def row_gather(tl, sl, r):
        return pltpu.make_async_copy(t_hbm.at[pl.ds(src_ref[tl * tm + r], 1)],
                                     tbuf.at[sl, pl.ds(r, 1)], gsem.at[sl])

    def row_scatter(tl, r):
        return pltpu.make_async_copy(acc_ref.at[pl.ds(r, 1)],
                                     y_hbm.at[pl.ds(src_ref[tl * tm + r], 1)], ssem)

    def for_rows(n, fn, **kw):
        def step(r, carry):
            fn(r)
            return carry
        lax.fori_loop(0, n, step, 0, **kw)

    def active(tl):
        return nv_ref[jnp.clip(tl, 0, n_tiles - 1)] > 0

    next_active = (tile + 1 < n_tiles) & active(tile + 1)

    @pl.when(active(tile))
    def _():
        @pl.when(chunk == 0)
        def _():
            @pl.when(tile == 0)
            def _():
                for_rows(tm, lambda r: row_gather(0, 0, r).start(), unroll=8)

            @pl.when(next_active)
            def _():
                for_rows(tm, lambda r: row_gather(tile + 1, 1 - slot, r).start(), unroll=8)

            @pl.when(tile > 0)
            def _():
                for_rows(nv_ref[jnp.maximum(tile - 1, 0)], lambda r: row_scatter(tile - 1, r).wait())

            acc_ref[...] = jnp.zeros_like(acc_ref)
            for_rows(tm, lambda r: row_gather(tile, slot, r).wait(), unroll=8)

        t = tbuf[slot, :, :d].astype(BF16)
        g = jnp.dot(t, jnp.concatenate([wg0_ref[...], wg1_ref[...]], axis=1), preferred_element_type=F32)
        u = jnp.dot(t, jnp.concatenate([wu0_ref[...], wu1_ref[...]], axis=1), preferred_element_type=F32)
        comb = tbuf[slot, :, d:]
        lane = lax.broadcasted_iota(jnp.int32, comb.shape, 1)
        lane0 = EXPERT_LANE0 + EXPERTS_PER_GROUP * grp_ref[tile]
        cw = [jnp.sum(jnp.where(lane == lane0 + e, comb, 0.0), axis=-1, keepdims=True)
              for e in range(EXPERTS_PER_GROUP)]
        col = chunk * fc + lax.broadcasted_iota(jnp.int32, (1, fc), 1)
        cw_col = cw[EXPERTS_PER_GROUP - 1]
        for e in range(EXPERTS_PER_GROUP - 2, -1, -1):
            cw_col = jnp.where(col < (e + 1) * ff, cw[e], cw_col)
        h = (g * jax.nn.sigmoid(g)) * u * cw_col
        acc_ref[...] += jnp.dot(h.astype(BF16), wd_ref[...], preferred_element_type=F32)

        @pl.when(chunk == n_chunks - 1)
        def _():
            n_valid = nv_ref[tile]

            @pl.when(n_valid == tm)
            def _():
                for_rows(tm, lambda r: row_scatter(tile, r).start(), unroll=8)

            @pl.when(n_valid < tm)
            def _():
                for_rows(n_valid, lambda r: row_scatter(tile, r).start())

            @pl.when(jnp.logical_not(next_active))
            def _():
                for_rows(n_valid, lambda r: row_scatter(tile, r).wait())


def moe_experts(t_ext, gid, wg, wu, wd, layer):
    m, dw = t_ext.shape
    d = dw - ROUTER_LANES
    n_layers, n_e, _, ff = wg.shape
    gff = EXPERTS_PER_GROUP * ff
    tiles_per_expert = ff // HEAD_DIM
    assert ff % HEAD_DIM == 0 and (EXPERTS_PER_GROUP * tiles_per_expert) % 2 == 0
    fc = 2 * HEAD_DIM
    tm = 512 if m % 512 == 0 and m >= 8192 else _pick(m, 256, 128)
    src, tile_grp, n_valid = _moe_plan(gid, tm)
    n_tiles = tile_grp.shape[0]

    def col_tile(k):
        def index_map(i, c, src, grp, nv):
            g = 2 * c + k
            return (layer, grp[i] * EXPERTS_PER_GROUP + g // tiles_per_expert, 0, g % tiles_per_expert)
        return pl.BlockSpec((None, None, d, HEAD_DIM), index_map)

    wd_grp = wd.reshape(n_layers, N_GROUPS, gff, d)
    grid_spec = pltpu.PrefetchScalarGridSpec(
        num_scalar_prefetch=3, grid=(n_tiles, gff // fc),
        in_specs=[pl.BlockSpec(memory_space=pl.ANY),
                  col_tile(0), col_tile(1), col_tile(0), col_tile(1),
                  pl.BlockSpec((None, None, fc, d), lambda i, c, src, grp, nv: (layer, grp[i], c, 0))],
        out_specs=pl.BlockSpec(memory_space=pl.ANY),
        scratch_shapes=[pltpu.VMEM((2, tm, dw), F32), pltpu.VMEM((tm, d), F32),
                        pltpu.SemaphoreType.DMA((2,)), pltpu.SemaphoreType.DMA(())])
    return pl.pallas_call(
        functools.partial(_moe_body, tm=tm, d=d, ff=ff, fc=fc), grid_spec=grid_spec,
        out_shape=jax.ShapeDtypeStruct((m, d), F32),
        compiler_params=_cparams("arbitrary", "arbitrary"), name="moe_experts",
    )(src, tile_grp, n_valid, t_ext, wg, wg, wu, wu, wd_grp)


def _router_weights(w_rg, b_rg, w_re, b_re):
    d = w_rg.shape[0]
    n_used = N_GROUPS + N_EXPERTS
    w = jnp.concatenate([w_rg, jnp.transpose(w_re, (1, 0, 2)).reshape(d, N_EXPERTS),
                         jnp.zeros((d, ROUTER_LANES - n_used), F32)], axis=1)
    bias = jnp.concatenate([b_rg, b_re.reshape(N_EXPERTS), jnp.zeros((ROUTER_LANES - n_used,), F32)])
    w_hi = w.astype(BF16)
    w_lo = (w - w_hi.astype(F32)).astype(BF16)
    return jnp.concatenate([w_hi, w_lo], axis=1), bias.reshape(1, ROUTER_LANES)


def kernel(x, c, ctx, c_ctx, ada_w, ada_b, norm_mix, w_in, sink_a, lambda_q1, lambda_k1, lambda_q2,
           lambda_k2, subln_b, q_norm_c, k_norm_c, w_out, norm_ffn, w_router_grp, b_router_grp,
           w_router_exp, b_router_exp, w_gate, w_up, w_down, final_norm):
    bx, n_lat, d = x.shape
    n_ctx = ctx.shape[1]
    depth = ada_w.shape[0]
    sizes, offs, _ = _proj_layout(d)

    n_cond = bx + 1
    rows = -(-n_cond // 8) * 8
    cond = jnp.concatenate([c, c_ctx[None], jnp.zeros((rows - n_cond, d), F32)], axis=0)
    mods = adaln_all(cond, ada_w, ada_b)

    tabs = (_rope_tables(n_lat, False), _rope_tables(n_lat, True))
    perms = _head_permutations(d)
    wg, wu, wd = w_gate.astype(BF16), w_up.astype(BF16), w_down.astype(BF16)

    xc = ctx
    pend = None
    pend_c = None
    for l in range(depth):
        last = l == depth - 1
        lam_init = 0.8 - 0.6 * math.exp(-0.3 * l)
        lat = [mods[l, :bx, None, i * d:(i + 1) * d] for i in range(N_MOD)]
        cm = [mods[l, bx:bx + 1, None, i * d:(i + 1) * d] for i in range(N_MOD)]
        w_in_l = cast_permute_proj(w_in, l, perms)
        gq = _permute_head_lanes(q_norm_c[l], _PERM_AC)
        gk = _permute_head_lanes(k_norm_c[l], _PERM_AC)
        w_out_l = w_out[l].astype(BF16)
        lams = (lambda_q1[l], lambda_k1[l], lambda_q2[l], lambda_k2[l])

        outs = norm_stage(x, norm_mix[l], res=pend, mod=(lat[0], lat[1]))
        if pend is not None:
            x, u = outs
        else:
            (u,) = outs
        outs = norm_stage(xc, norm_mix[l], res=pend_c, mod=(cm[0], cm[1]))
        if pend_c is not None:
            xc, uc = outs
        else:
            (uc,) = outs

        p = in_projection(u, w_in_l, gq, gk, rope_tabs=tabs)
        pc = in_projection(uc, w_in_l, gq, gk)

        oa = attention_a(p, p, pc, sink_a[l], d, local=True)
        ob = flash_attention("b", p, p, pc, d, lams=lams, gain=subln_b[l], lam_init=lam_init, tq=1024,
                             n_chains=2)
        oc = flash_attention("c", p, p, pc, d, tq=1024, n_chains=6)
        x = out_projection((oa, ob, oc), w_out_l, x, lat[2])

        if not last:
            oa_c = attention_a(pc, pc, pc, sink_a[l], d, local=False)
            ob_c = flash_attention("b", pc, None, pc, d, lams=lams, gain=subln_b[l], lam_init=lam_init)
            oc_c = flash_attention("c", pc, None, pc, d)
            xc = out_projection((oa_c, ob_c, oc_c), w_out_l, xc, cm[2])

        router = _router_weights(w_router_grp[l], b_router_grp[l], w_router_exp[l], b_router_exp[l])
        t, gid = norm_stage(x, norm_ffn[l], mod=(lat[3], lat[4]), router=router)
        y = moe_experts(t.reshape(bx * n_lat, d + ROUTER_LANES), gid[:, :, 0].reshape(bx * n_lat),
                        wg, wu, wd, l)
        pend = (y.reshape(bx, n_lat, d), lat[5])
        if not last:
            tc, gid_c = norm_stage(xc, norm_ffn[l], mod=(cm[3], cm[4]), router=router)
            yc = moe_experts(tc.reshape(bx * n_ctx, d + ROUTER_LANES), gid_c[:, :, 0].reshape(bx * n_ctx),
                             wg, wu, wd, l)
            pend_c = (yc.reshape(bx, n_ctx, d), cm[5])

    (out,) = norm_stage(x, final_norm, res=pend, keep_x=False, out_dtype=F32)
    return out
```

```python
import functools
import math

import jax
import jax.numpy as jnp
from jax import lax
from jax.experimental import pallas as pl
from jax.experimental.pallas import tpu as pltpu

F32 = jnp.float32
BF16 = jnp.bfloat16

HEAD_DIM = 128
GRID_W = 64
WINDOW = 128
ROPE_THETA = 10000.0
ATT_SCALE = HEAD_DIM ** -0.5
DIFF_SCALE = (HEAD_DIM // 2) ** -0.5
LOG2E = math.log2(math.e)
N_GROUPS = 4
EXPERTS_PER_GROUP = 4
N_EXPERTS = N_GROUPS * EXPERTS_PER_GROUP
N_MOD = 6
NORM_EPS = 1e-6
NEG_INF = -1e30
ROUTER_LANES = 128
EXPERT_LANE0 = N_GROUPS

V7X_VMEM_LIMIT_BYTES = 52 * 1024 * 1024


def _cparams(*sem):
    return pltpu.CompilerParams(dimension_semantics=sem, vmem_limit_bytes=V7X_VMEM_LIMIT_BYTES)


def _pick(n, *cands):
    for c in cands:
        if n % c == 0:
            return c
    return n


def _adaln_body(cond_ref, w_ref, b_ref, o_ref):
    c = cond_ref[...]
    s = (c * jax.nn.sigmoid(c)).astype(BF16)
    o_ref[...] = jnp.dot(s, w_ref[...].astype(BF16), preferred_element_type=F32) + b_ref[...]


def adaln_all(cond, ada_w, ada_b):
    n_layers, d, n6 = ada_w.shape
    r = cond.shape[0]
    tn = _pick(n6, 512, 256, 128)
    return pl.pallas_call(
        _adaln_body,
        grid=(n_layers, n6 // tn),
        in_specs=[
            pl.BlockSpec((r, d), lambda l, j: (0, 0)),
            pl.BlockSpec((None, d, tn), lambda l, j: (l, 0, j)),
            pl.BlockSpec((None, 1, tn), lambda l, j: (l, 0, j)),
        ],
        out_specs=pl.BlockSpec((None, r, tn), lambda l, j: (l, 0, j)),
        out_shape=jax.ShapeDtypeStruct((n_layers, r, n6), F32),
        compiler_params=_cparams("arbitrary", "arbitrary"),
        name="adaln",
    )(cond, ada_w, ada_b.reshape(n_layers, 1, n6))


def _norm_body(*refs, has_res, keep_x, has_mod, has_router):
    it = iter(refs)
    x_ref = next(it)
    if has_res:
        y_ref, gate_ref = next(it), next(it)
    g_ref = next(it)
    if has_mod:
        sh_ref, sc_ref = next(it), next(it)
    if has_router:
        whl_ref, rb_ref = next(it), next(it)
    if keep_x:
        xo_ref = next(it)
    u_ref = next(it)
    if has_router:
        gid_ref = next(it)

    x = x_ref[...]
    if has_res:
        x = x + gate_ref[...] * y_ref[...].astype(F32)
    if keep_x:
        xo_ref[...] = x
    xn = x * lax.rsqrt(jnp.mean(x * x, axis=-1, keepdims=True) + NORM_EPS) * g_ref[...]
    if has_mod:
        xn = xn * (1.0 + sc_ref[...]) + sh_ref[...]
    if not has_router:
        u_ref[...] = xn.astype(u_ref.dtype)
    else:
        d = xn.shape[-1]
        u_ref[:, :d] = xn
        th = xn.astype(BF16)
        tl = (xn - th.astype(F32)).astype(BF16)
        hi = jnp.dot(th, whl_ref[...], preferred_element_type=F32)
        logits = (hi[:, :ROUTER_LANES] + hi[:, ROUTER_LANES:]
                  + jnp.dot(tl, whl_ref[:, :ROUTER_LANES], preferred_element_type=F32)) + rb_ref[...]
        lane = lax.broadcasted_iota(jnp.int32, logits.shape, 1)
        big = jnp.int32(ROUTER_LANES)
        lg = jnp.where(lane < N_GROUPS, logits, -jnp.inf)
        gmax = jnp.max(lg, axis=-1, keepdims=True)
        p_grp = 1.0 / jnp.sum(jnp.exp(lg - gmax), axis=-1, keepdims=True)
        gidx = jnp.min(jnp.where(lg == gmax, lane, big), axis=-1, keepdims=True)
        e_lo = EXPERT_LANE0 + EXPERTS_PER_GROUP * gidx
        le = jnp.where((lane >= e_lo) & (lane < e_lo + EXPERTS_PER_GROUP), logits, -jnp.inf)
        l1 = jnp.max(le, axis=-1, keepdims=True)
        i1 = jnp.min(jnp.where(le == l1, lane, big), axis=-1, keepdims=True)
        le2 = jnp.where(lane == i1, -jnp.inf, le)
        l2 = jnp.max(le2, axis=-1, keepdims=True)
        i2 = jnp.min(jnp.where(le2 == l2, lane, big), axis=-1, keepdims=True)
        e2 = jnp.exp(l2 - l1)
        w1 = p_grp / (1.0 + e2)
        w2 = w1 * e2
        comb = jnp.where(lane == i1, w1, jnp.where(lane == i2, w2, 0.0))
        u_ref[:, d:] = comb
        gid_ref[...] = jnp.broadcast_to(gidx, gid_ref.shape)


def norm_stage(x, gain, *, res=None, keep_x=None, mod=None, router=None, out_dtype=BF16):
    keep_x = (res is not None) if keep_x is None else keep_x
    b, t, d = x.shape
    ts = _pick(t, 256, 128, 64, 32, 16, 8)
    row = pl.BlockSpec((None, ts, d), lambda bi, i: (bi, i, 0))

    def per_batch(arr):
        if arr.shape[0] == b:
            return pl.BlockSpec((None, 1, d), lambda bi, i: (bi, 0, 0))
        return pl.BlockSpec((None, 1, d), lambda bi, i: (0, 0, 0))

    args, specs = [x], [row]
    if res is not None:
        y, gate = res
        args += [y, gate]
        specs += [row, per_batch(gate)]
    args.append(gain.reshape(1, d))
    specs.append(pl.BlockSpec((1, d), lambda bi, i: (0, 0)))
    if mod is not None:
        args += [mod[0], mod[1]]
        specs += [per_batch(mod[0]), per_batch(mod[1])]
    if router is not None:
        args += list(router)
        specs += [pl.BlockSpec((d, 2 * ROUTER_LANES), lambda bi, i: (0, 0)),
                  pl.BlockSpec((1, ROUTER_LANES), lambda bi, i: (0, 0))]
    out_shapes, out_specs = [], []
    if keep_x:
        out_shapes.append(jax.ShapeDtypeStruct((b, t, d), F32))
        out_specs.append(row)
    if router is None:
        out_shapes.append(jax.ShapeDtypeStruct((b, t, d), out_dtype))
        out_specs.append(row)
    else:
        out_shapes.append(jax.ShapeDtypeStruct((b, t, d + ROUTER_LANES), F32))
        out_specs.append(pl.BlockSpec((None, ts, d + ROUTER_LANES), lambda bi, i: (bi, i, 0)))
        out_shapes.append(jax.ShapeDtypeStruct((b, t, ROUTER_LANES), jnp.int32))
        out_specs.append(pl.BlockSpec((None, ts, ROUTER_LANES), lambda bi, i: (bi, i, 0)))
    body = functools.partial(_norm_body, has_res=res is not None, keep_x=keep_x, has_mod=mod is not None,
                             has_router=router is not None)
    return pl.pallas_call(
        body, grid=(b, t // ts), in_specs=specs, out_specs=out_specs, out_shape=out_shapes,
        compiler_params=_cparams("arbitrary", "arbitrary"), name="norm_stage",
    )(*args)


_PERM_AC = ((2, 2, HEAD_DIM // 4), (1, 0, 2))
_PERM_B = ((2, 2, 2, HEAD_DIM // 8), (2, 0, 1, 3))


def _permute_head_lanes(arr, perm):
    split, order = perm
    lead = arr.shape[:-1]
    nl = len(lead)
    out = arr.reshape(lead + split).transpose(tuple(range(nl)) + tuple(nl + o for o in order))
    return out.reshape(lead + (HEAD_DIM,))


def _head_permutations(d):
    sizes, _, _ = _proj_layout(d)
    eye = jnp.eye(HEAD_DIM, dtype=BF16)
    mats = []
    for name, n in sizes.items():
        p = eye if name[0] == "v" else _permute_head_lanes(eye, _PERM_B if name[1] == "b" else _PERM_AC)
        mats.append(jnp.broadcast_to(p, (n, HEAD_DIM, HEAD_DIM)))
    return jnp.concatenate(mats, axis=0)


def _cast_permute_body(w_ref, p_ref, o_ref):
    w = w_ref[...].astype(BF16)
    for h in range(p_ref.shape[0]):
        sl = slice(h * HEAD_DIM, (h + 1) * HEAD_DIM)
        o_ref[:, sl] = jnp.dot(w[:, sl], p_ref[h], preferred_element_type=F32).astype(BF16)


def cast_permute_proj(w_all, layer, perms):
    _, d, pw = w_all.shape
    tr = _pick(d, 1024, 512, 256, 128)
    hpt = _pick(pw // HEAD_DIM, 4, 2, 1)
    tn = hpt * HEAD_DIM
    return pl.pallas_call(
        _cast_permute_body, grid=(d // tr, pw // tn),
        in_specs=[pl.BlockSpec((None, tr, tn), lambda i, j: (layer, i, j)),
                  pl.BlockSpec((hpt, HEAD_DIM, HEAD_DIM), lambda i, j: (j, 0, 0))],
        out_specs=pl.BlockSpec((tr, tn), lambda i, j: (i, j)),
        out_shape=jax.ShapeDtypeStruct((d, pw), BF16),
        compiler_params=_cparams("arbitrary", "arbitrary"), name="cast_permute_proj",
    )(w_all, perms)


def _rope_tables(seq, group_b):
    lane = jnp.arange(HEAD_DIM)
    if group_b:
        axis, f, block = (lane % 32) // 16, lane % 16, HEAD_DIM // 4
    else:
        axis, f, block = (lane % 64) // 32, lane % 32, HEAD_DIM // 2
    inv = ROPE_THETA ** (-(2.0 * f.astype(F32)) / block)
    t = jnp.arange(seq)
    pos = jnp.where((axis == 0)[None, :], (t // GRID_W)[:, None], (t % GRID_W)[:, None])
    ang = pos.astype(F32) * inv[None, :]
    return jnp.stack([jnp.cos(ang), jnp.where((lane < HEAD_DIM // 2)[None, :], -1.0, 1.0) * jnp.sin(ang)])


def _inproj_body(*refs, segs, rope, heads_per_tile):
    if rope:
        a_ref, w_ref, gq_ref, gk_ref, tab_ac_ref, tab_b_ref, o_ref, acc_ref = refs
    else:
        a_ref, w_ref, gq_ref, gk_ref, o_ref, acc_ref = refs
    j = pl.program_id(1)

    norm_lo = min(lo for lo, _, kind in segs if kind in ("qc", "kc"))
    norm_hi = max(hi for _, hi, kind in segs if kind in ("qc", "kc"))

    @pl.when((j >= norm_lo) & (j < norm_hi))
    def _():
        acc_ref[...] = jnp.dot(a_ref[...], w_ref[...], preferred_element_type=F32)

    def emit(kind):
        normed = kind in ("qc", "kc")
        sub = heads_per_tile if normed else min(heads_per_tile, 2)
        for c in range(heads_per_tile // sub):
            c0 = c * sub * HEAD_DIM
            if not normed:
                acc = jnp.dot(a_ref[...], w_ref[:, c0:c0 + sub * HEAD_DIM], preferred_element_type=F32)
            for h in range(sub):
                y = acc_ref[:, h * HEAD_DIM:(h + 1) * HEAD_DIM] if normed else acc[:, h * HEAD_DIM:(h + 1) * HEAD_DIM]
                if kind in ("qc", "kc"):
                    g = gq_ref[...] if kind == "qc" else gk_ref[...]
                    y = y * lax.rsqrt(jnp.mean(y * y, axis=-1, keepdims=True) + NORM_EPS) * g
                if rope and kind != "v":
                    tab = tab_b_ref if kind in ("qb", "kb") else tab_ac_ref
                    y = y * tab[0] + pltpu.roll(y, HEAD_DIM // 2, 1) * tab[1]
                if kind in ("qa", "qc"):
                    y = y * (ATT_SCALE * LOG2E)
                elif kind == "qb":
                    y = y * (DIFF_SCALE * LOG2E)
                o_ref[:, c0 + h * HEAD_DIM:c0 + (h + 1) * HEAD_DIM] = y.astype(o_ref.dtype)

    for lo, hi, kind in segs:
        pl.when((j >= lo) & (j < hi))(functools.partial(emit, kind))


def _proj_layout(d):
    h = d // HEAD_DIM
    a_h = 3 * h // 8
    a_kv = a_h // 3
    b_h = h // 4
    c_h = h - a_h - b_h
    c_kv = c_h // 3
    sizes = [a_h, a_kv, a_kv, b_h, b_h, b_h, c_h, c_kv, c_kv]
    names = ["qa", "ka", "va", "qb", "kb", "vb", "qc", "kc", "vc"]
    offs, acc = {}, 0
    for n, s in zip(names, sizes):
        offs[n] = acc
        acc += s
    return dict(zip(names, sizes)), offs, acc


def in_projection(u, w_bf16, q_gain, k_gain, *, rope_tabs=None):
    b, t, d = u.shape
    pw = w_bf16.shape[1]
    sizes, offs, total = _proj_layout(d)
    assert total * HEAD_DIM == pw
    m = b * t
    tm = _pick(t, 1024, 512, 256, 128) if rope_tabs is not None else _pick(m, 1024, 512, 256, 128)
    g = functools.reduce(math.gcd, sizes.values())
    hpt = _pick(g, 4, 2, 1)
    tn = hpt * HEAD_DIM
    kinds = {"qa": "qa", "ka": "ka", "va": "v", "qb": "qb", "kb": "kb", "vb": "v",
             "qc": "qc", "kc": "kc", "vc": "v"}
    segs = tuple((offs[n] // hpt, (offs[n] + sizes[n]) // hpt, kinds[n]) for n in sizes)
    in_specs = [
        pl.BlockSpec((tm, d), lambda i, j: (i, 0)),
        pl.BlockSpec((d, tn), lambda i, j: (0, j)),
        pl.BlockSpec((1, HEAD_DIM), lambda i, j: (0, 0)),
        pl.BlockSpec((1, HEAD_DIM), lambda i, j: (0, 0)),
    ]
    args = [u.reshape(m, d), w_bf16, q_gain.reshape(1, HEAD_DIM), k_gain.reshape(1, HEAD_DIM)]
    if rope_tabs is not None:
        nt = t // tm
        tab_spec = pl.BlockSpec((2, tm, HEAD_DIM), lambda i, j: (0, i % nt, 0))
        in_specs += [tab_spec, tab_spec]
        args += list(rope_tabs)
    body = functools.partial(_inproj_body, segs=segs, rope=rope_tabs is not None, heads_per_tile=hpt)
    out = pl.pallas_call(
        body, grid=(m // tm, pw // tn), in_specs=in_specs,
        out_specs=pl.BlockSpec((tm, tn), lambda i, j: (i, j)),
        out_shape=jax.ShapeDtypeStruct((m, pw), BF16),
        scratch_shapes=[pltpu.VMEM((tm, tn), F32)],
        compiler_params=_cparams("arbitrary", "arbitrary"), name="in_projection",
    )(*args)
    return out.reshape(b, t, pw)


def _attn_a_body(sink_ref, q0_ref, q1_ref, q2_ref, kl_ref, vl_ref, kc_ref, vc_ref, o_ref, vaug_ref,
                 *, local, tq, wlen):
    hkv = pl.program_id(1)
    qi = pl.program_id(2)
    n_lat = kl_ref.shape[0] if local else 0
    n_ctx = kc_ref.shape[0]

    @pl.when(qi == 0)
    def _():
        if local:
            vaug_ref[0:n_lat, 0:HEAD_DIM] = vl_ref[...]
        vaug_ref[n_lat:n_lat + n_ctx, 0:HEAD_DIM] = vc_ref[...]
        vaug_ref[:, HEAD_DIM:] = jnp.ones((n_lat + n_ctx, HEAD_DIM), BF16)

    nt = (((1,), (1,)), ((), ()))
    if local:
        start = pl.multiple_of(jnp.clip(qi * tq - WINDOW, 0, n_lat - wlen), WINDOW)
        k_win = kl_ref[pl.ds(start, wlen), :]
        v_win = vaug_ref[pl.ds(start, wlen), :]
        kpos = start + lax.broadcasted_iota(jnp.int32, (1, wlen), 1)
        qpos = qi * tq + lax.broadcasted_iota(jnp.int32, (tq, 1), 0)
        in_window = jnp.abs(kpos - qpos) <= WINDOW
    for g, q_ref in enumerate((q0_ref, q1_ref, q2_ref)):
        q = q_ref[...]
        m = jnp.full((tq, 1), LOG2E * sink_ref[3 * hkv + g], F32)
        acc = jnp.concatenate([jnp.zeros((tq, HEAD_DIM), F32), jnp.ones((tq, HEAD_DIM), F32)], axis=1)
        steps = [(kc_ref[...], vaug_ref[n_lat:n_lat + n_ctx, :], None)]
        if local:
            steps.append((k_win, v_win, in_window))
        for k, v, mask in steps:
            s = lax.dot_general(q, k, nt, preferred_element_type=F32)
            if mask is not None:
                s = jnp.where(mask, s, NEG_INF)
            m_new = jnp.maximum(m, jnp.max(s, axis=-1, keepdims=True))
            p = jnp.exp2(s - m_new)
            acc = jnp.exp2(m - m_new) * acc + jnp.dot(p.astype(BF16), v, preferred_element_type=F32)
            m = m_new
        o = acc[:, :HEAD_DIM] / acc[:, HEAD_DIM:]
        o_ref[:, g * HEAD_DIM:(g + 1) * HEAD_DIM] = o.astype(o_ref.dtype)


def attention_a(p_q, p_lat, p_ctx, sink, d, *, local):
    sizes, offs, _ = _proj_layout(d)
    b, t, _ = p_q.shape
    n_ctx = p_ctx.shape[1]
    n_lat = p_lat.shape[1]
    tq = _pick(t, *[c for c in (512, 256, 128) if not local or c + 2 * WINDOW <= n_lat])
    wlen = tq + 2 * WINDOW
    if local:
        assert n_lat >= wlen
    grid = (b, sizes["ka"], t // tq)

    def qspec(g):
        return pl.BlockSpec((None, tq, HEAD_DIM), lambda bi, h, i: (bi, i, offs["qa"] + 3 * h + g))

    def kvspec(n, name):
        return pl.BlockSpec((None, n, HEAD_DIM), lambda bi, h, i: (bi, 0, offs[name] + h))

    body = functools.partial(_attn_a_body, local=local, tq=tq, wlen=wlen)
    return pl.pallas_call(
        body, grid=grid,
        in_specs=[pl.BlockSpec(memory_space=pltpu.SMEM), qspec(0), qspec(1), qspec(2),
                  kvspec(n_lat, "ka"), kvspec(n_lat, "va"), kvspec(n_ctx, "ka"), kvspec(n_ctx, "va")],
        out_specs=pl.BlockSpec((None, tq, 3 * HEAD_DIM), lambda bi, h, i: (bi, i, h)),
        out_shape=jax.ShapeDtypeStruct((b, t, sizes["qa"] * HEAD_DIM), BF16),
        scratch_shapes=[pltpu.VMEM((n_ctx + (n_lat if local else 0), 2 * HEAD_DIM), BF16)],
        compiler_params=_cparams("arbitrary", "arbitrary", "arbitrary"), name="attention_a",
    )(sink, p_q, p_q, p_q, p_lat, p_lat, p_ctx, p_ctx)


def _flash_body(*refs, kind, has_lat, tq, kv_chunk, lam_init, n_chains):
    it = iter(refs)
    if kind == "b":
        lq1_ref, lk1_ref, lq2_ref, lk2_ref, gain_ref = (next(it) for _ in range(5))
        q = next(it)[...]
        lane = lax.broadcasted_iota(jnp.int32, q.shape, 1)
        sub1 = (lane % (HEAD_DIM // 2)) < HEAD_DIM // 4
        zero = jnp.zeros_like(q)
        q = jnp.concatenate([jnp.where(sub1, q, zero), jnp.where(sub1, zero, q)], axis=0)
    else:
        q = jnp.concatenate([next(it)[...], next(it)[...], next(it)[...]], axis=0)
    if has_lat:
        kl_ref, vl_ref = next(it), next(it)
    kc_ref, vc_ref, o_ref, vaug_ref = next(it), next(it), next(it), next(it)

    n_lat = kl_ref.shape[0] if has_lat else 0
    n_ctx = kc_ref.shape[0]

    @pl.when(pl.program_id(2) == 0)
    def _():
        if has_lat:
            vaug_ref[0:n_lat, 0:HEAD_DIM] = vl_ref[...]
        vaug_ref[n_lat:n_lat + n_ctx, 0:HEAD_DIM] = vc_ref[...]
        vaug_ref[:, HEAD_DIM:] = jnp.ones((n_lat + n_ctx, HEAD_DIM), BF16)

    chunks = [(kl_ref, lo, lo, min(kv_chunk, n_lat - lo)) for lo in range(0, n_lat, kv_chunk)]
    chunks.append((kc_ref, 0, n_lat, n_ctx))

    rows = q.shape[0] // n_chains
    qs = [q[c * rows:(c + 1) * rows] for c in range(n_chains)]
    ms = [jnp.full((rows, 1), -jnp.inf, F32) for _ in range(n_chains)]
    accs = [jnp.zeros((rows, 2 * HEAD_DIM), F32) for _ in range(n_chains)]
    for k_ref, lo, vlo, n in chunks:
        for c in range(n_chains):
            s = lax.dot_general(qs[c], k_ref[lo:lo + n, :], (((1,), (1,)), ((), ())),
                                preferred_element_type=F32)
            m_new = jnp.maximum(ms[c], jnp.max(s, axis=-1, keepdims=True))
            p = jnp.exp2(s - m_new)
            accs[c] = jnp.exp2(ms[c] - m_new) * accs[c] + jnp.dot(
                p.astype(BF16), vaug_ref[vlo:vlo + n, :], preferred_element_type=F32)
            ms[c] = m_new
    acc = jnp.concatenate(accs, axis=0)
    o = acc[:, :HEAD_DIM] / acc[:, HEAD_DIM:]
    if kind == "b":
        lam = (jnp.exp(jnp.sum(lq1_ref[...] * lk1_ref[...], axis=-1, keepdims=True))
               - jnp.exp(jnp.sum(lq2_ref[...] * lk2_ref[...], axis=-1, keepdims=True)) + lam_init)
        o = o[:tq] - lam * o[tq:]
        o = o * lax.rsqrt(jnp.mean(o * o, axis=-1, keepdims=True) + NORM_EPS) * gain_ref[...]
        o_ref[...] = (o * (1.0 - lam_init)).astype(o_ref.dtype)
    else:
        for g in range(3):
            o_ref[:, g * HEAD_DIM:(g + 1) * HEAD_DIM] = o[g * tq:(g + 1) * tq].astype(o_ref.dtype)


def flash_attention(kind, p_q, p_lat, p_ctx, d, *, lams=None, gain=None, lam_init=0.0, tq=512, kv_chunk=1024,
                    n_chains=1):
    sizes, offs, _ = _proj_layout(d)
    b, t, _ = p_q.shape
    tq = _pick(t, tq, 128)
    qn, kn, vn = "q" + kind, "k" + kind, "v" + kind
    n_kv = sizes[kn]
    args, specs = [], []
    if kind == "b":
        half = HEAD_DIM // 2
        args += [v.reshape(1, half) for v in lams] + [gain.reshape(1, HEAD_DIM)]
        specs += [pl.BlockSpec((1, half), lambda bi, h, i: (0, 0))] * 4
        specs += [pl.BlockSpec((1, HEAD_DIM), lambda bi, h, i: (0, 0))]
        args.append(p_q)
        specs.append(pl.BlockSpec((None, tq, HEAD_DIM), lambda bi, h, i: (bi, i, offs[qn] + h)))
        out_w = HEAD_DIM
    else:
        for g in range(3):
            args.append(p_q)
            specs.append(pl.BlockSpec((None, tq, HEAD_DIM),
                                      lambda bi, h, i, g=g: (bi, i, offs[qn] + 3 * h + g)))
        out_w = 3 * HEAD_DIM

    def kv(arr, name):
        return pl.BlockSpec((None, arr.shape[1], HEAD_DIM), lambda bi, h, i: (bi, 0, offs[name] + h))

    if p_lat is not None:
        args += [p_lat, p_lat]
        specs += [kv(p_lat, kn), kv(p_lat, vn)]
    args += [p_ctx, p_ctx]
    specs += [kv(p_ctx, kn), kv(p_ctx, vn)]
    n_keys = p_ctx.shape[1] + (p_lat.shape[1] if p_lat is not None else 0)
    body = functools.partial(_flash_body, kind=kind, has_lat=p_lat is not None, tq=tq,
                             kv_chunk=kv_chunk, lam_init=lam_init, n_chains=n_chains)
    return pl.pallas_call(
        body, grid=(b, n_kv, t // tq), in_specs=specs,
        out_specs=pl.BlockSpec((None, tq, out_w), lambda bi, h, i: (bi, i, h)),
        out_shape=jax.ShapeDtypeStruct((b, t, sizes[qn] * HEAD_DIM), BF16),
        scratch_shapes=[pltpu.VMEM((n_keys, 2 * HEAD_DIM), BF16)],
        compiler_params=_cparams("arbitrary", "arbitrary", "arbitrary"), name="flash_" + kind,
    )(*args)


def _outproj_body(*refs, n_parts):
    a_refs, w_refs = refs[:n_parts], refs[n_parts:2 * n_parts]
    x_ref, gate_ref, o_ref = refs[2 * n_parts:]
    acc = jnp.dot(a_refs[0][...], w_refs[0][...], preferred_element_type=F32)
    for a_ref, w_ref in zip(a_refs[1:], w_refs[1:]):
        acc += jnp.dot(a_ref[...], w_ref[...], preferred_element_type=F32)
    o_ref[...] = x_ref[...] + gate_ref[...] * acc


def out_projection(parts, w_bf16, x, gate):
    b, t, d = x.shape
    m = b * t
    per_batch = gate.shape[0] == b
    tm = _pick(t, 1024, 512, 256, 128) if per_batch else _pick(m, 1024, 512, 256, 128)
    tn = _pick(d, 512, 256, 128)
    nt = t // tm if per_batch else 1
    gate_map = (lambda i, j: (i // nt, 0, j)) if per_batch else (lambda i, j: (0, 0, j))
    widths = [p.shape[-1] for p in parts]
    w_parts, lo = [], 0
    for k in widths:
        w_parts.append(w_bf16[lo:lo + k])
        lo += k
    assert lo == w_bf16.shape[0]
    out = pl.pallas_call(
        functools.partial(_outproj_body, n_parts=len(parts)), grid=(m // tm, d // tn),
        in_specs=([pl.BlockSpec((tm, k), lambda i, j: (i, 0)) for k in widths]
                  + [pl.BlockSpec((k, tn), lambda i, j: (0, j)) for k in widths]
                  + [pl.BlockSpec((tm, tn), lambda i, j: (i, j)),
                     pl.BlockSpec((None, 1, tn), gate_map)]),
        out_specs=pl.BlockSpec((tm, tn), lambda i, j: (i, j)),
        out_shape=jax.ShapeDtypeStruct((m, d), F32),
        compiler_params=_cparams("arbitrary", "arbitrary"), name="out_projection",
    )(*[p.reshape(m, k) for p, k in zip(parts, widths)], *w_parts, x.reshape(m, d), gate)
    return out.reshape(b, t, d)


def _moe_plan(gid, tm):
    n = gid.shape[0]
    n_tiles = n // tm + N_GROUPS - 1
    onehot = (gid[:, None] == jnp.arange(N_GROUPS, dtype=jnp.int32)[None, :]).astype(jnp.int32)
    cnt = jnp.sum(onehot, axis=0)
    tiles_g = (cnt + tm - 1) // tm
    tile_end = jnp.cumsum(tiles_g)
    tile_start = tile_end - tiles_g
    rank = jnp.take_along_axis(jnp.cumsum(onehot, axis=0), gid[:, None], axis=1)[:, 0] - 1
    slot = tile_start[gid] * tm + rank
    src = jnp.zeros((n_tiles * tm,), jnp.int32).at[slot].set(jnp.arange(n, dtype=jnp.int32))
    tiles = jnp.arange(n_tiles, dtype=jnp.int32)
    tile_grp = jnp.minimum(jnp.sum((tile_end[None, :] <= tiles[:, None]).astype(jnp.int32), axis=1),
                           N_GROUPS - 1)
    n_valid = jnp.clip(cnt[tile_grp] - (tiles - tile_start[tile_grp]) * tm, 0, tm).astype(jnp.int32)
    return src, tile_grp, n_valid


def _moe_body(src_ref, grp_ref, nv_ref, t_hbm, wg0_ref, wg1_ref, wu0_ref, wu1_ref, wd_ref, y_hbm,
              tbuf, acc_ref, gsem, ssem, *, tm, d, ff, fc):
    tile = pl.program_id(0)
    chunk = pl.program_id(1)
    n_tiles = pl.num_programs(0)
    n_chunks = pl.num_programs(1)
    slot = tile % 2

    def row_gather(tl, sl, r):
        return pltpu.make_async_copy(t_hbm.at[pl.ds(src_ref[tl * tm + r], 1)],
                                     tbuf.at[sl, pl.ds(r, 1)], gsem.at[sl])

    def row_scatter(tl, r):
        return pltpu.make_async_copy(acc_ref.at[pl.ds(r, 1)],
                                     y_hbm.at[pl.ds(src_ref[tl * tm + r], 1)], ssem)

    def for_rows(n, fn, **kw):
        def step(r, carry):
            fn(r)
            return carry
        lax.fori_loop(0, n, step, 0, **kw)

    def active(tl):
        return nv_ref[jnp.clip(tl, 0, n_tiles - 1)] > 0

    next_active = (tile + 1 < n_tiles) & active(tile + 1)

    @pl.when(active(tile))
    def _():
        @pl.when(chunk == 0)
        def _():
            @pl.when(tile == 0)
            def _():
                for_rows(tm, lambda r: row_gather(0, 0, r).start(), unroll=8)

            @pl.when(next_active)
            def _():
                for_rows(tm, lambda r: row_gather(tile + 1, 1 - slot, r).start(), unroll=8)

            n_prev = nv_ref[jnp.maximum(tile - 1, 0)]

            @pl.when((tile > 0) & (n_prev == tm))
            def _():
                for_rows(tm, lambda r: row_scatter(tile - 1, r).wait(), unroll=8)

            @pl.when((tile > 0) & (n_prev < tm))
            def _():
                for_rows(n_prev, lambda r: row_scatter(tile - 1, r).wait())

            acc_ref[...] = jnp.zeros_like(acc_ref)
            for_rows(tm, lambda r: row_gather(tile, slot, r).wait(), unroll=8)

        t = tbuf[slot, :, :d].astype(BF16)
        g = jnp.dot(t, jnp.concatenate([wg0_ref[...], wg1_ref[...]], axis=1), preferred_element_type=F32)
        u = jnp.dot(t, jnp.concatenate([wu0_ref[...], wu1_ref[...]], axis=1), preferred_element_type=F32)
        comb = tbuf[slot, :, d:]
        lane = lax.broadcasted_iota(jnp.int32, comb.shape, 1)
        lane0 = EXPERT_LANE0 + EXPERTS_PER_GROUP * grp_ref[tile]
        cw = [jnp.sum(jnp.where(lane == lane0 + e, comb, 0.0), axis=-1, keepdims=True)
              for e in range(EXPERTS_PER_GROUP)]
        col = chunk * fc + lax.broadcasted_iota(jnp.int32, (1, fc), 1)
        cw_col = cw[EXPERTS_PER_GROUP - 1]
        for e in range(EXPERTS_PER_GROUP - 2, -1, -1):
            cw_col = jnp.where(col < (e + 1) * ff, cw[e], cw_col)
        h = (g * jax.nn.sigmoid(g)) * u * cw_col
        acc_ref[...] += jnp.dot(h.astype(BF16), wd_ref[...], preferred_element_type=F32)

        @pl.when(chunk == n_chunks - 1)
        def _():
            n_valid = nv_ref[tile]

            @pl.when(n_valid == tm)
            def _():
                for_rows(tm, lambda r: row_scatter(tile, r).start(), unroll=8)

            @pl.when(n_valid < tm)
            def _():
                for_rows(n_valid, lambda r: row_scatter(tile, r).start())

            @pl.when(jnp.logical_not(next_active))
            def _():
                for_rows(n_valid, lambda r: row_scatter(tile, r).wait())


def moe_experts(t_ext, gid, wg, wu, wd, layer):
    m, dw = t_ext.shape
    d = dw - ROUTER_LANES
    n_layers, n_e, _, ff = wg.shape
    gff = EXPERTS_PER_GROUP * ff
    tiles_per_expert = ff // HEAD_DIM
    assert ff % HEAD_DIM == 0 and (EXPERTS_PER_GROUP * tiles_per_expert) % 2 == 0
    fc = 2 * HEAD_DIM
    tm = 512 if m % 512 == 0 and m >= 8192 else _pick(m, 256, 128)
    src, tile_grp, n_valid = _moe_plan(gid, tm)
    n_tiles = tile_grp.shape[0]

    def col_tile(k):
        def index_map(i, c, src, grp, nv):
            g = 2 * c + k
            return (layer, grp[i] * EXPERTS_PER_GROUP + g // tiles_per_expert, 0, g % tiles_per_expert)
        return pl.BlockSpec((None, None, d, HEAD_DIM), index_map)

    wd_grp = wd.reshape(n_layers, N_GROUPS, gff, d)
    grid_spec = pltpu.PrefetchScalarGridSpec(
        num_scalar_prefetch=3, grid=(n_tiles, gff // fc),
        in_specs=[pl.BlockSpec(memory_space=pl.ANY),
                  col_tile(0), col_tile(1), col_tile(0), col_tile(1),
                  pl.BlockSpec((None, None, fc, d), lambda i, c, src, grp, nv: (layer, grp[i], c, 0))],
        out_specs=pl.BlockSpec(memory_space=pl.ANY),
        scratch_shapes=[pltpu.VMEM((2, tm, dw), F32), pltpu.VMEM((tm, d), F32),
                        pltpu.SemaphoreType.DMA((2,)), pltpu.SemaphoreType.DMA(())])
    return pl.pallas_call(
        functools.partial(_moe_body, tm=tm, d=d, ff=ff, fc=fc), grid_spec=grid_spec,
        out_shape=jax.ShapeDtypeStruct((m, d), F32),
        compiler_params=_cparams("arbitrary", "arbitrary"), name="moe_experts",
    )(src, tile_grp, n_valid, t_ext, wg, wg, wu, wu, wd_grp)


def _router_weights(w_rg, b_rg, w_re, b_re):
    d = w_rg.shape[0]
    n_used = N_GROUPS + N_EXPERTS
    w = jnp.concatenate([w_rg, jnp.transpose(w_re, (1, 0, 2)).reshape(d, N_EXPERTS),
                         jnp.zeros((d, ROUTER_LANES - n_used), F32)], axis=1)
    bias = jnp.concatenate([b_rg, b_re.reshape(N_EXPERTS), jnp.zeros((ROUTER_LANES - n_used,), F32)])
    w_hi = w.astype(BF16)
    w_lo = (w - w_hi.astype(F32)).astype(BF16)
    return jnp.concatenate([w_hi, w_lo], axis=1), bias.reshape(1, ROUTER_LANES)


def kernel(x, c, ctx, c_ctx, ada_w, ada_b, norm_mix, w_in, sink_a, lambda_q1, lambda_k1, lambda_q2,
           lambda_k2, subln_b, q_norm_c, k_norm_c, w_out, norm_ffn, w_router_grp, b_router_grp,
           w_router_exp, b_router_exp, w_gate, w_up, w_down, final_norm):
    bx, n_lat, d = x.shape
    n_ctx = ctx.shape[1]
    depth = ada_w.shape[0]
    sizes, offs, _ = _proj_layout(d)

    n_cond = bx + 1
    rows = -(-n_cond // 8) * 8
    cond = jnp.concatenate([c, c_ctx[None], jnp.zeros((rows - n_cond, d), F32)], axis=0)
    mods = adaln_all(cond, ada_w, ada_b)

    tabs = (_rope_tables(n_lat, False), _rope_tables(n_lat, True))
    perms = _head_permutations(d)
    wg, wu, wd = w_gate.astype(BF16), w_up.astype(BF16), w_down.astype(BF16)

    xc = ctx
    pend = None
    pend_c = None
    for l in range(depth):
        last = l == depth - 1
        lam_init = 0.8 - 0.6 * math.exp(-0.3 * l)
        lat = [mods[l, :bx, None, i * d:(i + 1) * d] for i in range(N_MOD)]
        cm = [mods[l, bx:bx + 1, None, i * d:(i + 1) * d] for i in range(N_MOD)]
        w_in_l = cast_permute_proj(w_in, l, perms)
        gq = _permute_head_lanes(q_norm_c[l], _PERM_AC)
        gk = _permute_head_lanes(k_norm_c[l], _PERM_AC)
        w_out_l = w_out[l].astype(BF16)
        lams = (lambda_q1[l], lambda_k1[l], lambda_q2[l], lambda_k2[l])

        outs = norm_stage(x, norm_mix[l], res=pend, mod=(lat[0], lat[1]))
        if pend is not None:
            x, u = outs
        else:
            (u,) = outs
        outs = norm_stage(xc, norm_mix[l], res=pend_c, mod=(cm[0], cm[1]))
        if pend_c is not None:
            xc, uc = outs
        else:
            (uc,) = outs

        p = in_projection(u, w_in_l, gq, gk, rope_tabs=tabs)
        pc = in_projection(uc, w_in_l, gq, gk)

        oa = attention_a(p, p, pc, sink_a[l], d, local=True)
        ob = flash_attention("b", p, p, pc, d, lams=lams, gain=subln_b[l], lam_init=lam_init, tq=1024,
                             n_chains=2)
        oc = flash_attention("c", p, p, pc, d, tq=1024, n_chains=6)
        x = out_projection((oa, ob, oc), w_out_l, x, lat[2])

        if not last:
            oa_c = attention_a(pc, pc, pc, sink_a[l], d, local=False)
            ob_c = flash_attention("b", pc, None, pc, d, lams=lams, gain=subln_b[l], lam_init=lam_init)
            oc_c = flash_attention("c", pc, None, pc, d)
            xc = out_projection((oa_c, ob_c, oc_c), w_out_l, xc, cm[2])

        router = _router_weights(w_router_grp[l], b_router_grp[l], w_router_exp[l], b_router_exp[l])
        t, gid = norm_stage(x, norm_ffn[l], mod=(lat[3], lat[4]), router=router)
        y = moe_experts(t.reshape(bx * n_lat, d + ROUTER_LANES), gid[:, :, 0].reshape(bx * n_lat),
                        wg, wu, wd, l)
        pend = (y.reshape(bx, n_lat, d), lat[5])
        if not last:
            tc, gid_c = norm_stage(xc, norm_ffn[l], mod=(cm[3], cm[4]), router=router)
            yc = moe_experts(tc.reshape(bx * n_ctx, d + ROUTER_LANES), gid_c[:, :, 0].reshape(bx * n_ctx),
                             wg, wu, wd, l)
            pend_c = (yc.reshape(bx, n_ctx, d), cm[5])

    (out,) = norm_stage(x, final_norm, res=pend, keep_x=False, out_dtype=F32)
    return out
```
